```python
import jax, jax.numpy as jnp
from jax import lax
import numpy as np

D_MODEL = 1024
BATCH = 16
SEQ = 2048
DEPTH = 1
DEC_BATCH = 32
DEC_SEQ = 1
PAST_LEN = 16384
PAGE_SIZE = 128

SB_HEADS = 8
SB_HEAD_DIM = 64
SB_WIDTH = SB_HEADS * SB_HEAD_DIM
SB_BLOCK = 128
SB_BIAS_INIT = -8.0
GLA_HEADS = 4
GLA_HEAD_K = 64
GLA_HEAD_V = 128
GLA_K_WIDTH = GLA_HEADS * GLA_HEAD_K
GLA_V_WIDTH = GLA_HEADS * GLA_HEAD_V
GLA_GATE_RANK = 16
GLA_TAU = 16.0
GLA_CHUNK = 64
D_FF = 4 * D_MODEL
ALPHA = (2.0 * DEPTH) ** 0.25
BETA = (8.0 * DEPTH) ** -0.25
LN_EPS = 1e-5
GN_EPS = 1e-6
IN_SPLITS = (SB_WIDTH, SB_WIDTH, SB_WIDTH, GLA_K_WIDTH, GLA_K_WIDTH, GLA_V_WIDTH, GLA_V_WIDTH,
             GLA_GATE_RANK, D_MODEL, D_MODEL)
N_IN = 3 * SB_WIDTH + 2 * GLA_K_WIDTH + 2 * GLA_V_WIDTH + GLA_GATE_RANK + 2 * D_MODEL

kernel_name = "stickbreak_gla_deepnorm_adaln_step"


def _ln_stats(x):
    xf = x.astype(jnp.float32)
    mu = jnp.mean(xf, axis=-1, keepdims=True)
    var = jnp.mean(jnp.square(xf - mu), axis=-1, keepdims=True)
    return (xf - mu) * lax.rsqrt(var + LN_EPS)


def layer_norm(x, g, b):
    return (_ln_stats(x) * g + b).astype(x.dtype)


def modulate(x, shift, scale):
    return (_ln_stats(x) * (1.0 + scale) + shift).astype(x.dtype)


def sb_attend(q, k, v, bias, q_offset):
    B, Tq, H, d = q.shape
    Tk = k.shape[1]
    blk = SB_BLOCK if Tq % SB_BLOCK == 0 else Tq
    nb = Tq // blk
    qb = jnp.moveaxis(q.reshape(B, nb, blk, H, d), 1, 0)
    kpos = jnp.arange(Tk)
    scale = d ** -0.5
    bias_f = bias.astype(jnp.float32)[None, :, None, None]

    def one_block(args):
        qi, i = args
        z = jnp.einsum("bqhd,bkhd->bhqk", qi, k, preferred_element_type=jnp.float32) * scale + bias_f
        qpos = q_offset + i * blk + jnp.arange(blk)
        mask = kpos[None, :] < qpos[:, None]
        log_keep = jnp.where(mask, -jax.nn.softplus(z), 0.0)
        rest = lax.cumsum(log_keep, axis=3, reverse=True) - log_keep
        w = jnp.where(mask, jnp.exp(jax.nn.log_sigmoid(z) + rest), 0.0)
        return jnp.einsum("bhqk,bkhd->bqhd", w, v.astype(jnp.float32)).astype(q.dtype)

    o = lax.map(one_block, (qb, jnp.arange(nb)))
    return jnp.moveaxis(o, 0, 1).reshape(B, Tq, H * d)


def gla_mix(q, k, v, log_a, S0):
    B, T, H, dk = q.shape
    dv = v.shape[-1]
    C = GLA_CHUNK if T % GLA_CHUNK == 0 else T
    N = T // C
    f32 = jnp.float32
    q = (q.astype(f32) * dk ** -0.5).reshape(B, N, C, H, dk)
    k = k.astype(f32).reshape(B, N, C, H, dk)
    v = v.astype(f32).reshape(B, N, C, H, dv)
    b = jnp.cumsum(log_a.reshape(B, N, C, H, dk), axis=2)
    b_last = b[:, :, -1:]
    q_dec = q * jnp.exp(b)
    k_inv = k * jnp.exp(-b)
    k_end = k * jnp.exp(b_last - b)
    causal = jnp.tril(jnp.ones((C, C), dtype=bool))
    att = jnp.where(causal, jnp.einsum("bnihk,bnjhk->bnhij", q_dec, k_inv), 0.0)
    o_intra = jnp.einsum("bnhij,bnjhv->bnihv", att, v)
    U = jnp.einsum("bnjhk,bnjhv->nbhkv", k_end, v)
    decay = jnp.moveaxis(jnp.exp(b_last[:, :, 0]), 1, 0)

    def step(S, inp):
        dcy, u = inp
        return dcy[..., None] * S + u, S

    S_T, S_prev = lax.scan(step, S0.astype(f32), (decay, U))
    o_inter = jnp.einsum("bnihk,nbhkv->bnihv", q_dec, S_prev)
    return (o_intra + o_inter).reshape(B, T, H, dv), S_T


def trunk_layer(x, c, past_k, past_v, S0, w_ada, b_ada, w_in, sb_bias, w_gla_g2, b_gla_g, gla_norm_g,
                w_up_a, w_up_b, w_o, ln1_g, ln1_b, w_ff1, w_ff2, ln2_g, ln2_b):
    B, T, _ = x.shape
    P = past_k.shape[1]
    ada = jax.nn.silu(c) @ w_ada + b_ada
    sh1, sc1, gt1, sh2, sc2, gt2 = [a[:, None, :] for a in jnp.split(ada, 6, axis=-1)]

    h = modulate(x, sh1, sc1)
    split_at = [int(s) for s in np.cumsum(IN_SPLITS)[:-1]]
    sb_q, sb_k, sb_v, g_q, g_k, g_v, g_r, g_low, br_a, br_b = jnp.split(h @ w_in, split_at, axis=-1)

    def heads(t, n):
        return t.reshape(B, T, n, -1)

    k_sb = heads(sb_k, SB_HEADS)
    v_sb = heads(sb_v, SB_HEADS)
    keys = jnp.concatenate([past_k.astype(k_sb.dtype), k_sb], axis=1)
    vals = jnp.concatenate([past_v.astype(v_sb.dtype), v_sb], axis=1)
    o_sb = sb_attend(heads(sb_q, SB_HEADS), keys, vals, sb_bias, P)

    log_a = jax.nn.log_sigmoid((g_low @ w_gla_g2 + b_gla_g).astype(jnp.float32)) / GLA_TAU
    o_g, S_T = gla_mix(heads(g_q, GLA_HEADS), heads(g_k, GLA_HEADS), heads(g_v, GLA_HEADS),
                       heads(log_a, GLA_HEADS), S0)
    o_g = o_g * lax.rsqrt(jnp.mean(jnp.square(o_g), axis=-1, keepdims=True) + GN_EPS) * gla_norm_g
    o_g = (o_g * jax.nn.silu(heads(g_r, GLA_HEADS).astype(jnp.float32))).reshape(B, T, GLA_V_WIDTH)
    o_g = o_g.astype(x.dtype)

    merged = jax.nn.sigmoid(br_a) * (o_sb @ w_up_a) + jax.nn.sigmoid(br_b) * (o_g @ w_up_b)
    x1 = layer_norm(ALPHA * x + gt1 * (merged @ w_o), ln1_g, ln1_b)

    h2 = modulate(x1, sh2, sc2)
    f = jnp.square(jax.nn.relu(h2 @ w_ff1)) @ w_ff2
    x2 = layer_norm(ALPHA * x1 + gt2 * f, ln2_g, ln2_b)
    return x2, k_sb, v_sb, S_T.astype(S0.dtype)


def setup_inputs(seed: int = 0) -> dict:
    key = jax.random.key(seed)
    ks = jax.random.split(key, 24)
    n_pages = PAST_LEN // PAGE_SIZE
    n_phys = (DEC_BATCH * n_pages * 5) // 4

    def nrm(k, shape, scale):
        return jax.random.normal(k, shape, jnp.float32) * scale

    page_table = jax.random.permutation(ks[0], n_phys)[: DEC_BATCH * n_pages]
    page_table = page_table.reshape(DEC_BATCH, n_pages).astype(jnp.int32)
    L = DEPTH
    return {
        "x_prompt": nrm(ks[1], (BATCH, SEQ, D_MODEL), 1.0),
        "x_sample": nrm(ks[2], (DEC_BATCH, DEC_SEQ, D_MODEL), 1.0),
        "cache_k": nrm(ks[3], (L, n_phys, PAGE_SIZE, SB_HEADS, SB_HEAD_DIM), 1.0),
        "cache_v": nrm(ks[4], (L, n_phys, PAGE_SIZE, SB_HEADS, SB_HEAD_DIM), 1.0),
        "state_gla": nrm(ks[5], (L, DEC_BATCH, GLA_HEADS, GLA_HEAD_K, GLA_HEAD_V), 1.0),
        "page_table": page_table,
        "c_prompt": nrm(ks[6], (BATCH, D_MODEL), 1.0),
        "c_sample": nrm(ks[7], (DEC_BATCH, D_MODEL), 1.0),
        "w_ada": nrm(ks[8], (L, D_MODEL, 6 * D_MODEL), 0.5 * D_MODEL ** -0.5),
        "b_ada": nrm(ks[9], (L, 6 * D_MODEL), 0.01),
        "w_in": nrm(ks[10], (L, D_MODEL, N_IN), D_MODEL ** -0.5),
        "sb_bias": SB_BIAS_INIT + nrm(ks[23], (L, SB_HEADS), 0.1),
        "w_gla_g2": nrm(ks[11], (L, GLA_GATE_RANK, GLA_K_WIDTH), GLA_GATE_RANK ** -0.5),
        "b_gla_g": nrm(ks[12], (L, GLA_K_WIDTH), 0.1),
        "gla_norm_g": 1.0 + nrm(ks[13], (L, GLA_HEAD_V), 0.02),
        "w_up_a": nrm(ks[14], (L, SB_WIDTH, D_MODEL), SB_WIDTH ** -0.5),
        "w_up_b": nrm(ks[15], (L, GLA_V_WIDTH, D_MODEL), GLA_V_WIDTH ** -0.5),
        "w_o": nrm(ks[16], (L, D_MODEL, D_MODEL), BETA * D_MODEL ** -0.5),
        "ln1_g": 1.0 + nrm(ks[17], (L, D_MODEL), 0.02),
        "ln1_b": nrm(ks[18], (L, D_MODEL), 0.01),
        "w_ff1": nrm(ks[19], (L, D_MODEL, D_FF), D_MODEL ** -0.5),
        "w_ff2": nrm(ks[20], (L, D_FF, D_MODEL), BETA * D_FF ** -0.5),
        "ln2_g": 1.0 + nrm(ks[21], (L, D_MODEL), 0.02),
        "ln2_b": nrm(ks[22], (L, D_MODEL), 0.01),
    }


def reference(x_prompt, x_sample, cache_k, cache_v, state_gla, page_table, c_prompt, c_sample,
              w_ada, b_ada, w_in, sb_bias, w_gla_g2, b_gla_g, gla_norm_g, w_up_a, w_up_b, w_o,
              ln1_g, ln1_b, w_ff1, w_ff2, ln2_g, ln2_b):
    n_pages = page_table.shape[1]
    past = n_pages * cache_k.shape[2]
    dec_b = x_sample.shape[0]
    pb = x_prompt.shape[0]
    yp, ys = x_prompt, x_sample
    kp_l, vp_l, sp_l, ks_l, vs_l, ss_l = [], [], [], [], [], []
    for l in range(DEPTH):
        wl = (w_ada[l], b_ada[l], w_in[l], sb_bias[l], w_gla_g2[l], b_gla_g[l], gla_norm_g[l], w_up_a[l],
              w_up_b[l], w_o[l], ln1_g[l], ln1_b[l], w_ff1[l], w_ff2[l], ln2_g[l], ln2_b[l])
        empty = jnp.zeros((pb, 0, SB_HEADS, SB_HEAD_DIM), x_prompt.dtype)
        s0 = jnp.zeros((pb, GLA_HEADS, GLA_HEAD_K, GLA_HEAD_V), state_gla.dtype)
        yp, kp, vp, sp = trunk_layer(yp, c_prompt, empty, empty, s0, *wl)
        pk = cache_k[l][page_table].reshape(dec_b, past, SB_HEADS, SB_HEAD_DIM)
        pv = cache_v[l][page_table].reshape(dec_b, past, SB_HEADS, SB_HEAD_DIM)
        ys, kss, vss, sss = trunk_layer(ys, c_sample, pk, pv, state_gla[l], *wl)
        kp_l.append(kp); vp_l.append(vp); sp_l.append(sp)
        ks_l.append(kss); vs_l.append(vss); ss_l.append(sss)
    new_k_prompt = jnp.stack(kp_l)
    new_v_prompt = jnp.stack(vp_l)
    new_state_prompt = jnp.stack(sp_l)
    new_k_sample = jnp.stack(ks_l)
    new_v_sample = jnp.stack(vs_l)
    new_state_sample = jnp.stack(ss_l)
    return (yp, ys, new_k_prompt, new_v_prompt, new_state_prompt, new_k_sample, new_v_sample, new_state_sample)
```

```python
import functools

import numpy as np
import jax
import jax.numpy as jnp
from jax import lax
from jax.experimental import pallas as pl
from jax.experimental.pallas import tpu as pltpu

F32 = jnp.float32
BF16 = jnp.bfloat16

SB_HEADS = 8
SB_HEAD_DIM = 64
SB_WIDTH = SB_HEADS * SB_HEAD_DIM
GLA_HEADS = 4
GLA_HEAD_K = 64
GLA_HEAD_V = 128
GLA_K_WIDTH = GLA_HEADS * GLA_HEAD_K
GLA_V_WIDTH = GLA_HEADS * GLA_HEAD_V
GLA_GATE_RANK = 16
GLA_TAU = 16.0
GLA_CHUNK = 64
LN_EPS = 1e-5
GN_EPS = 1e-6

LANES = 128
VMEM_LIMIT_BYTES = 56 * 1024 * 1024

TOKEN_TILE = 512
SB_Q_TILE = 256
SB_K_TILE = 128
GLA_BLOCK = 256
PAGES_PER_STEP = 16
FF_CHUNK = 1024


def _cparams(*sem):
    return pltpu.CompilerParams(dimension_semantics=sem, vmem_limit_bytes=VMEM_LIMIT_BYTES)


def _dot(a, b):
    return jnp.dot(a, b, preferred_element_type=F32)


def _dot_nt(a, b):
    return lax.dot_general(a, b, (((1,), (1,)), ((), ())), preferred_element_type=F32)


def _ln_stats(x):
    mu = jnp.mean(x, axis=-1, keepdims=True)
    xc = x - mu
    var = jnp.mean(xc * xc, axis=-1, keepdims=True)
    return xc * lax.rsqrt(var + LN_EPS)


def _log_sigmoid(x):
    return jnp.minimum(x, 0.0) - jnp.log1p(jnp.exp(-jnp.abs(x)))


def _sigmoid(x):
    return 1.0 / (1.0 + jnp.exp(-x))


def _div_pow2(x, n):
    assert n & (n - 1) == 0
    return jnp.right_shift(x, n.bit_length() - 1)


def _split2(x):
    hi = x.astype(BF16)
    lo = (x - hi.astype(F32)).astype(BF16)
    return hi, lo


def _split3(x):
    hi = x.astype(BF16)
    r = x - hi.astype(F32)
    mid = r.astype(BF16)
    lo = (r - mid.astype(F32)).astype(BF16)
    return hi, mid, lo


def _ada_kernel(c_ref, w_ref, b_ref, o_ref):
    c = c_ref[...]
    a = c * _sigmoid(c)
    o_ref[...] = _dot(a.astype(BF16), w_ref[...]) + b_ref[...]


def _ada(c_all, w_ada_b, b_ada):
    n, d = c_all.shape
    nout = w_ada_b.shape[1]
    blk = d
    return pl.pallas_call(
        _ada_kernel,
        grid=(nout // blk,),
        in_specs=[pl.BlockSpec((n, d), lambda j: (0, 0)),
                  pl.BlockSpec((d, blk), lambda j: (0, j)),
                  pl.BlockSpec((1, blk), lambda j: (0, j))],
        out_specs=pl.BlockSpec((n, blk), lambda j: (0, j)),
        out_shape=jax.ShapeDtypeStruct((n, nout), F32),
        compiler_params=_cparams("parallel"),
        name="ada",
    )(c_all, w_ada_b, b_ada.reshape(1, nout))


def _proj_math(x, sh1, sc1, wm_ref, wl_ref, wg2_ref, bg_ref):
    h = (_ln_stats(x) * (1.0 + sc1) + sh1).astype(BF16)
    y = _dot(h, wm_ref[...])
    g_low = _dot(h, wl_ref[...])
    gl = _dot(g_low.astype(BF16), wg2_ref[...]) + bg_ref[...]
    log_a = _log_sigmoid(gl) * (1.0 / GLA_TAU)
    return y, log_a


def _proj_prompt_kernel(x_ref, mod_ref, wm_ref, wl_ref, wg2_ref, bg_ref,
                        qT_ref, k_ref, v_ref, kb_ref, vT_ref, gq_ref, gk_ref, gv_ref, gr_ref, la_ref):
    y, log_a = _proj_math(x_ref[...], mod_ref[0, 0], mod_ref[0, 1], wm_ref, wl_ref, wg2_ref, bg_ref)
    w = SB_WIDTH
    q = y[:, 0:w] * (SB_HEAD_DIM ** -0.5)
    k = y[:, w:2 * w]
    v = y[:, 2 * w:3 * w]
    k_ref[...] = k
    v_ref[...] = v
    kb_ref[...] = k.astype(BF16)
    qT = q.T.astype(BF16)
    vT = v.T.astype(BF16)
    for c in range(qT_ref.shape[0]):
        qT_ref[c] = qT[:, c * SB_Q_TILE:(c + 1) * SB_Q_TILE]
    for c in range(vT_ref.shape[0]):
        vT_ref[c] = vT[:, c * SB_K_TILE:(c + 1) * SB_K_TILE]
    o = 3 * w
    gq_ref[...] = y[:, o:o + GLA_K_WIDTH]
    gk_ref[...] = y[:, o + GLA_K_WIDTH:o + 2 * GLA_K_WIDTH]
    o += 2 * GLA_K_WIDTH
    gv_ref[...] = y[:, o:o + GLA_V_WIDTH].astype(BF16)
    gr_ref[...] = y[:, o + GLA_V_WIDTH:o + 2 * GLA_V_WIDTH]
    la_ref[...] = log_a


def _proj_sample_kernel(x_ref, mod_ref, wm_ref, wl_ref, wg2_ref, bg_ref,
                        q_ref, k_ref, v_ref, gq_ref, gk_ref, gv_ref, gr_ref, la_ref):
    y, log_a = _proj_math(x_ref[...], mod_ref[0, 0], mod_ref[0, 1], wm_ref, wl_ref, wg2_ref, bg_ref)
    w = SB_WIDTH
    q_ref[...] = y[:, 0:w] * (SB_HEAD_DIM ** -0.5)
    k_ref[...] = y[:, w:2 * w]
    v_ref[...] = y[:, 2 * w:3 * w]
    o = 3 * w
    gq_ref[...] = y[:, o:o + GLA_K_WIDTH]
    gk_ref[...] = y[:, o + GLA_K_WIDTH:o + 2 * GLA_K_WIDTH]
    o += 2 * GLA_K_WIDTH
    gv_ref[...] = y[:, o:o + GLA_V_WIDTH]
    gr_ref[...] = y[:, o + GLA_V_WIDTH:o + 2 * GLA_V_WIDTH]
    la_ref[...] = log_a


def _const_spec(shape):
    nd = len(shape)
    return pl.BlockSpec(shape, lambda *a: (0,) * nd)


def _proj_prompt(x2d, mod, seq_len, wm, wl, wg2, bg):
    n, d = x2d.shape
    tm = min(TOKEN_TILE, seq_len)
    assert seq_len % tm == 0 and tm % SB_Q_TILE == 0
    tiles_per_seq = seq_len // tm
    row = lambda cols: pl.BlockSpec((tm, cols), lambda i: (i, 0))
    out_shape = (
        jax.ShapeDtypeStruct((n // SB_Q_TILE, SB_WIDTH, SB_Q_TILE), BF16),
        jax.ShapeDtypeStruct((n, SB_WIDTH), F32),
        jax.ShapeDtypeStruct((n, SB_WIDTH), F32),
        jax.ShapeDtypeStruct((n, SB_WIDTH), BF16),
        jax.ShapeDtypeStruct((n // SB_K_TILE, SB_WIDTH, SB_K_TILE), BF16),
        jax.ShapeDtypeStruct((n, GLA_K_WIDTH), F32),
        jax.ShapeDtypeStruct((n, GLA_K_WIDTH), F32),
        jax.ShapeDtypeStruct((n, GLA_V_WIDTH), BF16),
        jax.ShapeDtypeStruct((n, GLA_V_WIDTH), F32),
        jax.ShapeDtypeStruct((n, GLA_K_WIDTH), F32),
    )
    out_specs = (
        pl.BlockSpec((tm // SB_Q_TILE, SB_WIDTH, SB_Q_TILE), lambda i: (i, 0, 0)),
        row(SB_WIDTH), row(SB_WIDTH), row(SB_WIDTH),
        pl.BlockSpec((tm // SB_K_TILE, SB_WIDTH, SB_K_TILE), lambda i: (i, 0, 0)),
        row(GLA_K_WIDTH), row(GLA_K_WIDTH), row(GLA_V_WIDTH), row(GLA_V_WIDTH), row(GLA_K_WIDTH),
    )
    return pl.pallas_call(
        _proj_prompt_kernel,
        grid=(n // tm,),
        in_specs=[row(d),
                  pl.BlockSpec((1, 6, 1, d), lambda i: (i // tiles_per_seq, 0, 0, 0)),
                  _const_spec(wm.shape), _const_spec(wl.shape), _const_spec(wg2.shape),
                  _const_spec(bg.shape)],
        out_specs=out_specs,
        out_shape=out_shape,
        compiler_params=_cparams("parallel"),
        name="proj_prompt",
    )(x2d, mod, wm, wl, wg2, bg)


def _proj_sample(x2d, mod, wm, wl, wg2, bg):
    n, d = x2d.shape
    full = lambda cols: pl.BlockSpec((n, cols), lambda i: (0, 0))
    widths = (SB_WIDTH, SB_WIDTH, SB_WIDTH, GLA_K_WIDTH, GLA_K_WIDTH, GLA_V_WIDTH, GLA_V_WIDTH,
              GLA_K_WIDTH)
    return pl.pallas_call(
        _proj_sample_kernel,
        grid=(1,),
        in_specs=[full(d), _const_spec(mod.shape),
                  _const_spec(wm.shape), _const_spec(wl.shape), _const_spec(wg2.shape),
                  _const_spec(bg.shape)],
        out_specs=tuple(full(c) for c in widths),
        out_shape=tuple(jax.ShapeDtypeStruct((n, c), F32) for c in widths),
        compiler_params=_cparams("arbitrary"),
        name="proj_sample",
    )(x2d, mod, wm, wl, wg2, bg)


def _sb_scan_matrix():
    k = SB_K_TILE
    s = np.arange(k)[:, None]
    j = np.arange(k)[None, :]
    upper = (j > s).astype(np.float32)
    top = np.concatenate([upper, upper], axis=1)
    ones = np.ones((16, 2 * k), np.float32)
    return jnp.asarray(np.concatenate([top, ones], axis=0), dtype=BF16)


def _sb_prompt_kernel(bias_ref, qT_ref, k_ref, vT_ref, a_ref, o_ref):
    hp = pl.program_id(1)
    i = pl.program_id(2)
    tq, tk, hd = SB_Q_TILE, SB_K_TILE, SB_HEAD_DIM
    scan = a_ref[...]
    qT = qT_ref[0].astype(F32)
    head_row = lax.broadcasted_iota(jnp.int32, (2 * hd, tq), 0)
    key_in_blk = lax.broadcasted_iota(jnp.int32, (tk, tq), 0)
    q_in_tile = lax.broadcasted_iota(jnp.int32, (tk, tq), 1)

    accs = []
    for h in range(2):
        in_head = (head_row < hd) if h == 0 else (head_row >= hd)
        qm = jnp.where(in_head, qT, 0.0).astype(BF16)
        bias = bias_ref[2 * hp + h]

        def block(kb, carry, acc, masked, qm=qm, bias=bias, h=h):
            kblk = k_ref[pl.ds(pl.multiple_of(kb * tk, tk), tk), :]
            z = _dot(kblk, qm) + bias
            soft = jnp.log1p(jnp.exp(-jnp.abs(z)))
            log_beta = jnp.minimum(z, 0.0) - soft
            log_keep = log_beta - z
            if masked:
                valid = (kb * tk + key_in_blk) < (i * tq + q_in_tile)
                log_keep = jnp.where(valid, log_keep, 0.0)
            hi, lo = _split2(log_keep)
            r = _dot(scan, jnp.concatenate([hi, lo], axis=0))
            w = jnp.exp(log_beta + r[:tk] + carry)
            if masked:
                w = jnp.where(valid, w, 0.0)
            acc = acc + _dot(vT_ref[kb, h * hd:(h + 1) * hd, :], w.astype(BF16))
            return carry + r[tk:tk + 1], acc

        carry = jnp.zeros((1, tq), F32)
        acc = jnp.zeros((hd, tq), F32)
        n_diag = tq // tk
        for d in range(n_diag):
            carry, acc = block(n_diag * i + (n_diag - 1 - d), carry, acc, True)

        def body(t, st):
            return block(n_diag * i - 1 - t, st[0], st[1], False)

        carry, acc = lax.fori_loop(0, n_diag * i, body, (carry, acc))
        accs.append(acc)

    oT = jnp.concatenate(accs, axis=0)
    o_ref[...] = oT.T.astype(o_ref.dtype)


def _sb_prompt(bias, qT3, k_bf, vT3, batch, seq_len):
    n = k_bf.shape[0]
    tq, tk = SB_Q_TILE, SB_K_TILE
    assert seq_len % tq == 0
    qt = seq_len // tq
    kblocks = seq_len // tk
    scan = _sb_scan_matrix()
    return pl.pallas_call(
        _sb_prompt_kernel,
        grid=(batch, SB_HEADS // 2, qt),
        in_specs=[pl.BlockSpec(memory_space=pltpu.SMEM),
                  pl.BlockSpec((1, 2 * SB_HEAD_DIM, tq), lambda b, hp, i: (b * qt + i, hp, 0)),
                  pl.BlockSpec((seq_len, 2 * SB_HEAD_DIM), lambda b, hp, i: (b, hp)),
                  pl.BlockSpec((kblocks, 2 * SB_HEAD_DIM, tk), lambda b, hp, i: (b, hp, 0)),
                  _const_spec(scan.shape)],
        out_specs=pl.BlockSpec((tq, 2 * SB_HEAD_DIM), lambda b, hp, i: (b * qt + i, hp)),
        out_shape=jax.ShapeDtypeStruct((n, SB_WIDTH), BF16),
        compiler_params=_cparams("parallel", "parallel", "arbitrary"),
        name="sb_prompt",
    )(bias, qT3, k_bf, vT3, scan)


def _gla_consts():
    t = np.arange(GLA_BLOCK)
    same = (t[:, None] // GLA_CHUNK) == (t[None, :] // GLA_CHUNK)
    tri = (same & (t[None, :] <= t[:, None])).astype(np.float32)
    ones = same.astype(np.float32)
    scan = np.concatenate([np.concatenate([tri] * 3, axis=1),
                           np.concatenate([ones] * 3, axis=1)], axis=0)
    n_chunks = GLA_BLOCK // GLA_CHUNK
    ind = np.zeros((GLA_BLOCK, n_chunks * LANES), np.float32)
    for c in range(n_chunks):
        ind[c * GLA_CHUNK:(c + 1) * GLA_CHUNK, c * LANES:(c + 1) * LANES] = 1.0
    ind = np.concatenate([ind] * 3, axis=0)
    return jnp.asarray(scan, dtype=BF16), jnp.asarray(ind, dtype=BF16)


def _gla_prompt_kernel(gq_ref, gk_ref, gv_ref, la_ref, gr_ref, ng_ref, scan_ref, ind_ref,
                       og_ref, st_ref, s_scr):
    j = pl.program_id(1)
    tb, c_len = GLA_BLOCK, GLA_CHUNK
    n_chunks = tb // c_len
    dk, dv, nh = GLA_HEAD_K, GLA_HEAD_V, GLA_HEADS

    @pl.when(j == 0)
    def _():
        s_scr[...] = jnp.zeros_like(s_scr)

    la = la_ref[...]
    parts = jnp.concatenate(_split3(la), axis=0)
    bb = _dot(scan_ref[...], parts)
    b = bb[:tb]
    b_last = bb[tb:]
    laT_parts = jnp.concatenate(_split3(la.T), axis=1)
    b_last_col = _dot(laT_parts, ind_ref[...])

    q = gq_ref[...] * (dk ** -0.5)
    k = gk_ref[...]
    v = gv_ref[...]
    q_dec = q * jnp.exp(b)
    k_inv = (k * jnp.exp(-b)).astype(BF16)
    k_endT = (k * jnp.exp(b_last - b)).T

    lane_head = _div_pow2(lax.broadcasted_iota(jnp.int32, (tb, nh * dk), 1), dk)
    ti = lax.broadcasted_iota(jnp.int32, (tb, tb), 0)
    tj = lax.broadcasted_iota(jnp.int32, (tb, tb), 1)
    tj_chunk = _div_pow2(tj, c_len)
    causal = (_div_pow2(ti, c_len) == tj_chunk) & (tj <= ti)

    qm = [jnp.where(lane_head == h, q_dec, 0.0).astype(BF16) for h in range(nh)]
    o_intra = []
    for h in range(nh):
        att = jnp.where(causal, _dot_nt(qm[h], k_inv), 0.0).astype(BF16)
        o_intra.append(_dot(att, v[:, h * dv:(h + 1) * dv]))

    s_all = s_scr[...]
    norm_g = ng_ref[...]
    for c in range(n_chunks):
        rows = slice(c * c_len, (c + 1) * c_len)
        q_stack = jnp.concatenate([qm[h][rows] for h in range(nh)], axis=0)
        o_inter = _dot(q_stack, s_all.astype(BF16))
        for h in range(nh):
            o = o_intra[h][rows] + o_inter[h * c_len:(h + 1) * c_len]
            o = o * lax.rsqrt(jnp.mean(o * o, axis=-1, keepdims=True) + GN_EPS) * norm_g
            r = gr_ref[rows, h * dv:(h + 1) * dv]
            og_ref[rows, h * dv:(h + 1) * dv] = (o * (r * _sigmoid(r))).astype(og_ref.dtype)
        k_endT_c = jnp.where(tj_chunk == c, k_endT, 0.0).astype(BF16)
        u = _dot(k_endT_c, v)
        u_diag = jnp.concatenate(
            [u[h * dk:(h + 1) * dk, h * dv:(h + 1) * dv] for h in range(nh)], axis=0)
        s_all = jnp.exp(b_last_col[:, c * LANES:(c + 1) * LANES]) * s_all + u_diag
    s_scr[...] = s_all

    @pl.when(j == pl.num_programs(1) - 1)
    def _():
        st_ref[0] = s_all


def _gla_prompt(gq, gk, gv, la, gr, norm_g, batch, seq_len):
    n = gq.shape[0]
    tb = GLA_BLOCK
    assert seq_len % tb == 0
    nb = seq_len // tb
    scan, ind = _gla_consts()
    row = lambda cols: pl.BlockSpec((tb, cols), lambda b, j: (b * nb + j, 0))
    rows_s = GLA_HEADS * GLA_HEAD_K
    return pl.pallas_call(
        _gla_prompt_kernel,
        grid=(batch, nb),
        in_specs=[row(GLA_K_WIDTH), row(GLA_K_WIDTH), row(GLA_V_WIDTH), row(GLA_K_WIDTH),
                  row(GLA_V_WIDTH), _const_spec((1, GLA_HEAD_V)),
                  _const_spec(scan.shape), _const_spec(ind.shape)],
        out_specs=(row(GLA_V_WIDTH),
                   pl.BlockSpec((1, rows_s, GLA_HEAD_V), lambda b, j: (b, 0, 0))),
        out_shape=(jax.ShapeDtypeStruct((n, GLA_V_WIDTH), BF16),
                   jax.ShapeDtypeStruct((batch, rows_s, GLA_HEAD_V), F32)),
        scratch_shapes=[pltpu.VMEM((rows_s, GLA_HEAD_V), F32)],
        compiler_params=_cparams("parallel", "arbitrary"),
        name="gla_prompt",
    )(gq, gk, gv, la, gr, norm_g.reshape(1, GLA_HEAD_V), scan, ind)


def _post_kernel(alpha, x_ref, osb_ref, og_ref, mod_ref, wg_ref, wua_ref, wub_ref, wo_ref,
                 ln1g_ref, ln1b_ref, wf1_ref, wf2_ref, ln2g_ref, ln2b_ref, y_ref):
    x = x_ref[...]
    sh1, sc1, gt1, sh2, sc2, gt2 = [mod_ref[0, m] for m in range(6)]
    d = x.shape[-1]
    h = (_ln_stats(x) * (1.0 + sc1) + sh1).astype(BF16)
    merged = (_sigmoid(_dot(h, wg_ref[:, :d])) * _dot(osb_ref[...].astype(BF16), wua_ref[...])
              + _sigmoid(_dot(h, wg_ref[:, d:])) * _dot(og_ref[...].astype(BF16), wub_ref[...]))
    mixed = _dot(merged.astype(BF16), wo_ref[...])
    x1 = _ln_stats(alpha * x + gt1 * mixed) * ln1g_ref[...] + ln1b_ref[...]
    h2 = (_ln_stats(x1) * (1.0 + sc2) + sh2).astype(BF16)
    f = jnp.zeros_like(x)
    for c in range(wf1_ref.shape[1] // FF_CHUNK):
        cols = slice(c * FF_CHUNK, (c + 1) * FF_CHUNK)
        u = jnp.maximum(_dot(h2, wf1_ref[:, cols]), 0.0)
        f = f + _dot((u * u).astype(BF16), wf2_ref[cols, :])
    y_ref[...] = _ln_stats(alpha * x1 + gt2 * f) * ln2g_ref[...] + ln2b_ref[...]


def _post(x2d, o_sb, o_g, mod, seq_len, alpha, wg, wua, wub, wo, ln1g, ln1b, wf1, wf2, ln2g, ln2b):
    n, d = x2d.shape
    per_token = mod.shape[2] != 1
    tm = n if per_token else min(TOKEN_TILE, seq_len)
    tiles_per_seq = 1 if per_token else seq_len // tm
    row = lambda cols: pl.BlockSpec((tm, cols), lambda i: (i, 0))
    single = lambda a: pl.BlockSpec(a.shape, lambda i: (0,) * a.ndim, pipeline_mode=pl.Buffered(1))
    vec = lambda a: a.reshape(1, d)
    mod_spec = (_const_spec(mod.shape) if per_token else
                pl.BlockSpec((1, 6, 1, d), lambda i: (i // tiles_per_seq, 0, 0, 0)))
    return pl.pallas_call(
        functools.partial(_post_kernel, alpha),
        grid=(n // tm,),
        in_specs=[row(d), row(SB_WIDTH), row(GLA_V_WIDTH), mod_spec,
                  single(wg), single(wua), single(wub), single(wo),
                  _const_spec((1, d)), _const_spec((1, d)),
                  single(wf1), single(wf2),
                  _const_spec((1, d)), _const_spec((1, d))],
        out_specs=row(d),
        out_shape=jax.ShapeDtypeStruct((n, d), F32),
        compiler_params=_cparams("parallel"),
        name="post",
    )(x2d, o_sb, o_g, mod, wg, wua, wub, wo, vec(ln1g), vec(ln1b), wf1, wf2, vec(ln2g), vec(ln2b))


def _paged_consts():
    p, nh = PAGES_PER_STEP, SB_HEADS
    lanes = LANES
    j = np.arange(lanes)[:, None]
    s = np.arange(lanes)[None, :]
    later = (j > s).astype(np.float32)
    half = np.concatenate([later, np.ones((lanes, lanes), np.float32)], axis=1)
    lane_scan = np.concatenate([half, half], axis=0)
    r = np.arange(p * nh)
    slot, head = r // nh, r % nh
    same_head = head[:, None] == head[None, :]
    newer = same_head & (slot[None, :] < slot[:, None])
    top = np.concatenate([newer, newer], axis=1).astype(np.float32)
    bot = np.concatenate([same_head, same_head], axis=1).astype(np.float32)
    page_scan = np.concatenate([top, bot], axis=0)
    return jnp.asarray(lane_scan, dtype=BF16), jnp.asarray(page_scan, dtype=BF16)


def _paged_kernel(pt_ref, bias_ref, q_ref, lscan_ref, pscan_ref, *refs):
    p, nh, hd = PAGES_PER_STEP, SB_HEADS, SB_HEAD_DIM
    k_refs = refs[:p]
    v_refs = refs[p:2 * p]
    o_ref = refs[2 * p]
    carry_scr, acc_scr = refs[2 * p + 1], refs[2 * p + 2]
    g = pl.program_id(1)
    rows = p * nh
    width = nh * hd

    @pl.when(g == 0)
    def _():
        carry_scr[...] = jnp.zeros_like(carry_scr)
        acc_scr[...] = jnp.zeros_like(acc_scr)

    q = q_ref[0]
    row16 = lax.broadcasted_iota(jnp.int32, (16, width), 0)
    lane_head16 = _div_pow2(lax.broadcasted_iota(jnp.int32, (16, width), 1), hd)
    own = row16 == lane_head16
    q_bd = jnp.where(own, jnp.broadcast_to(q, (16, width)), 0.0).astype(BF16)

    zs = []
    for s in range(p):
        kp = k_refs[s][0].astype(BF16)
        zs.append(_dot_nt(q_bd, kp)[:nh])
    z = jnp.concatenate(zs, axis=0)
    head_of_row = jnp.bitwise_and(lax.broadcasted_iota(jnp.int32, (rows, LANES), 0), nh - 1)
    bias = jnp.zeros((rows, LANES), F32)
    for h in range(nh):
        bias = jnp.where(head_of_row == h, bias_ref[h], bias)
    z = z + bias
    soft = jnp.log1p(jnp.exp(-jnp.abs(z)))
    log_beta = jnp.minimum(z, 0.0) - soft
    log_keep = log_beta - z
    hi, lo = _split2(log_keep)
    r1 = _dot(jnp.concatenate([hi, lo], axis=1), lscan_ref[...])
    rest_in = r1[:, :LANES]
    row_tot = r1[:, LANES:]
    thi, tlo = _split2(row_tot)
    r2 = _dot(pscan_ref[...], jnp.concatenate([thi, tlo], axis=0))
    carry = carry_scr[...]
    w = jnp.exp(log_beta + rest_in + r2[:rows] + carry)
    carry_scr[...] = carry + r2[rows:]

    acc = acc_scr[...]
    pad = jnp.zeros((16 - nh, LANES), F32)
    for s in range(p):
        w_s = jnp.concatenate([w[s * nh:(s + 1) * nh], pad], axis=0).astype(BF16)
        acc = acc + _dot(w_s, v_refs[s][0].astype(BF16))
    acc_scr[...] = acc

    @pl.when(g == pl.num_programs(1) - 1)
    def _():
        o_ref[0] = jnp.sum(jnp.where(own, acc, 0.0), axis=0, keepdims=True)


def _paged_attention(page_table, bias, q, cache_k3, cache_v3):
    nseq, n_pages = page_table.shape
    p = PAGES_PER_STEP
    assert n_pages % p == 0 and cache_k3.shape[1] == LANES
    groups = n_pages // p
    width = SB_WIDTH
    lscan, pscan = _paged_consts()

    def page_spec(s):
        def imap(b, g, pt):
            return (pt[b, n_pages - 1 - (g * p + s)], 0, 0)
        return pl.BlockSpec((1, cache_k3.shape[1], width), imap)

    smem = pl.BlockSpec(memory_space=pltpu.SMEM)
    grid_spec = pltpu.PrefetchScalarGridSpec(
        num_scalar_prefetch=1,
        grid=(nseq, groups),
        in_specs=[smem,
                  pl.BlockSpec((1, 1, width), lambda b, g, pt: (b, 0, 0)),
                  pl.BlockSpec(lscan.shape, lambda b, g, pt: (0, 0)),
                  pl.BlockSpec(pscan.shape, lambda b, g, pt: (0, 0))]
                 + [page_spec(s) for s in range(p)] + [page_spec(s) for s in range(p)],
        out_specs=pl.BlockSpec((1, 1, width), lambda b, g, pt: (b, 0, 0)),
        scratch_shapes=[pltpu.VMEM((p * SB_HEADS, LANES), F32), pltpu.VMEM((16, width), F32)],
    )
    out = pl.pallas_call(
        _paged_kernel,
        grid_spec=grid_spec,
        out_shape=jax.ShapeDtypeStruct((nseq, 1, width), F32),
        compiler_params=_cparams("parallel", "arbitrary"),
        name="sb_paged",
    )(page_table, bias, q.reshape(nseq, 1, width), lscan, pscan,
      *([cache_k3] * p), *([cache_v3] * p))
    return out.reshape(nseq, width)


def _gla_sample_kernel(gq_ref, gk_ref, la_ref, gv_ref, gr_ref, ng_ref, s0_ref, og_ref, st_ref):
    nh, dk, dv = GLA_HEADS, GLA_HEAD_K, GLA_HEAD_V
    la = la_ref[0]
    q_dec = gq_ref[0] * (dk ** -0.5) * jnp.exp(la)
    k_inv = gk_ref[0] * jnp.exp(-la)
    k_end = gk_ref[0]
    s0 = s0_ref[0]
    v8 = gv_ref[0]
    qk = q_dec * k_inv
    qs = q_dec * s0
    outs = []
    for h in range(nh):
        rows = slice(h * dk, (h + 1) * dk)
        att = jnp.sum(qk[rows], axis=0, keepdims=True)
        outs.append(att * v8[h:h + 1] + jnp.sum(qs[rows], axis=0, keepdims=True))
    o = jnp.concatenate(outs + [jnp.zeros((8 - nh, dv), F32)], axis=0)
    o = o * lax.rsqrt(jnp.mean(o * o, axis=-1, keepdims=True) + GN_EPS) * ng_ref[...]
    r = gr_ref[0]
    og_ref[0] = o * (r * _sigmoid(r))
    v_rows = jnp.concatenate(
        [jnp.broadcast_to(v8[h:h + 1], (dk, dv)) for h in range(nh)], axis=0)
    st_ref[0] = jnp.exp(la) * s0 + k_end * v_rows


def _gla_sample(gq, gk, gv, la, gr, norm_g, s0):
    n = gq.shape[0]
    nh, dk, dv = GLA_HEADS, GLA_HEAD_K, GLA_HEAD_V
    kw = nh * dk
    pad_heads = lambda a: jnp.pad(a.reshape(n, nh, dv), ((0, 0), (0, 8 - nh), (0, 0)))
    colspec = pl.BlockSpec((1, kw, 1), lambda b: (b, 0, 0))
    headspec = pl.BlockSpec((1, 8, dv), lambda b: (b, 0, 0))
    statespec = pl.BlockSpec((1, kw, dv), lambda b: (b, 0, 0))
    og, st = pl.pallas_call(
        _gla_sample_kernel,
        grid=(n,),
        in_specs=[colspec, colspec, colspec, headspec, headspec, _const_spec((1, dv)), statespec],
        out_specs=(headspec, statespec),
        out_shape=(jax.ShapeDtypeStruct((n, 8, dv), F32),
                   jax.ShapeDtypeStruct((n, kw, dv), F32)),
        compiler_params=_cparams("parallel"),
        name="gla_sample",
    )(gq.reshape(n, kw, 1), gk.reshape(n, kw, 1), la.reshape(n, kw, 1),
      pad_heads(gv), pad_heads(gr), norm_g.reshape(1, dv), s0)
    return og[:, :nh].reshape(n, nh * dv), st


def kernel(x_prompt, x_sample, cache_k, cache_v, state_gla, page_table, c_prompt, c_sample,
           w_ada, b_ada, w_in, sb_bias, w_gla_g2, b_gla_g, gla_norm_g, w_up_a, w_up_b, w_o,
           ln1_g, ln1_b, w_ff1, w_ff2, ln2_g, ln2_b):
    depth = w_ada.shape[0]
    pb, seq_len, d = x_prompt.shape
    sb_n = x_sample.shape[0]
    assert x_sample.shape[1] == 1
    alpha = (2.0 * depth) ** 0.25
    n_phys, page_size = cache_k.shape[1], cache_k.shape[2]

    yp = x_prompt.reshape(pb * seq_len, d)
    ys = x_sample.reshape(sb_n, d)
    c_all = jnp.concatenate([c_prompt, c_sample], axis=0)
    main_w = 3 * SB_WIDTH + 2 * GLA_K_WIDTH + 2 * GLA_V_WIDTH
    gate_off = main_w + GLA_GATE_RANK

    outs = [[] for _ in range(6)]
    for l in range(depth):
        ada = _ada(c_all, w_ada[l].astype(BF16), b_ada[l])
        mod_p = ada[:pb].reshape(pb, 6, 1, d)
        mod_s = ada[pb:].reshape(sb_n, 6, d).transpose(1, 0, 2).reshape(1, 6, sb_n, d)

        w_in_l = w_in[l]
        wm = w_in_l[:, :main_w].astype(BF16)
        wl = jnp.pad(w_in_l[:, main_w:gate_off], ((0, 0), (0, LANES - GLA_GATE_RANK))).astype(BF16)
        wg = w_in_l[:, gate_off:].astype(BF16)
        wg2 = jnp.pad(w_gla_g2[l], ((0, LANES - GLA_GATE_RANK), (0, 0))).astype(BF16)
        bg = b_gla_g[l].reshape(1, GLA_K_WIDTH)
        post_w = (wg, w_up_a[l].astype(BF16), w_up_b[l].astype(BF16), w_o[l].astype(BF16),
                  ln1_g[l], ln1_b[l], w_ff1[l].astype(BF16), w_ff2[l].astype(BF16),
                  ln2_g[l], ln2_b[l])

        qT3, k_p, v_p, k_bf, vT3, gq, gk, gv, gr, la = _proj_prompt(yp, mod_p, seq_len, wm, wl, wg2, bg)
        o_sb = _sb_prompt(sb_bias[l], qT3, k_bf, vT3, pb, seq_len)
        o_g, st_p = _gla_prompt(gq, gk, gv, la, gr, gla_norm_g[l], pb, seq_len)
        yp = _post(yp, o_sb, o_g, mod_p, seq_len, alpha, *post_w)

        q_s, k_s, v_s, gq_s, gk_s, gv_s, gr_s, la_s = _proj_sample(ys, mod_s, wm, wl, wg2, bg)
        o_sb_s = _paged_attention(page_table, sb_bias[l], q_s,
                                  cache_k[l].reshape(n_phys, page_size, SB_WIDTH),
                                  cache_v[l].reshape(n_phys, page_size, SB_WIDTH))
        s0 = state_gla[l].reshape(sb_n, GLA_HEADS * GLA_HEAD_K, GLA_HEAD_V)
        o_g_s, st_s = _gla_sample(gq_s, gk_s, gv_s, la_s, gr_s, gla_norm_g[l], s0)
        ys = _post(ys, o_sb_s, o_g_s, mod_s, 1, alpha, *post_w)

        outs[0].append(k_p.reshape(pb, seq_len, SB_HEADS, SB_HEAD_DIM))
        outs[1].append(v_p.reshape(pb, seq_len, SB_HEADS, SB_HEAD_DIM))
        outs[2].append(st_p.reshape(pb, GLA_HEADS, GLA_HEAD_K, GLA_HEAD_V))
        outs[3].append(k_s.reshape(sb_n, 1, SB_HEADS, SB_HEAD_DIM))
        outs[4].append(v_s.reshape(sb_n, 1, SB_HEADS, SB_HEAD_DIM))
        outs[5].append(st_s.reshape(sb_n, GLA_HEADS, GLA_HEAD_K, GLA_HEAD_V))

    stacked = [jnp.stack(o) for o in outs]
    return (yp.reshape(pb, seq_len, d), ys.reshape(sb_n, 1, d),
            stacked[0], stacked[1], stacked[2].astype(state_gla.dtype),
            stacked[3], stacked[4], stacked[5].astype(state_gla.dtype))
```

```python
import functools

import numpy as np
import jax
import jax.numpy as jnp
from jax import lax
from jax.experimental import pallas as pl
from jax.experimental.pallas import tpu as pltpu

F32 = jnp.float32
BF16 = jnp.bfloat16

SB_HEADS = 8
SB_HEAD_DIM = 64
SB_WIDTH = SB_HEADS * SB_HEAD_DIM
GLA_HEADS = 4
GLA_HEAD_K = 64
GLA_HEAD_V = 128
GLA_K_WIDTH = GLA_HEADS * GLA_HEAD_K
GLA_V_WIDTH = GLA_HEADS * GLA_HEAD_V
GLA_GATE_RANK = 16
GLA_TAU = 16.0
GLA_CHUNK = 64
LN_EPS = 1e-5
GN_EPS = 1e-6

LANES = 128
VMEM_LIMIT_BYTES = 56 * 1024 * 1024

TOKEN_TILE = 512
SB_Q_TILE = 256
SB_K_TILE = 128
SB_HEADS_PER_STEP = 4
GLA_BLOCK = 256
PAGES_PER_STEP = 16
FF_CHUNK = 1024


def _cparams(*sem):
    return pltpu.CompilerParams(dimension_semantics=sem, vmem_limit_bytes=VMEM_LIMIT_BYTES)


def _dot(a, b):
    return jnp.dot(a, b, preferred_element_type=F32)


def _dot_nt(a, b):
    return lax.dot_general(a, b, (((1,), (1,)), ((), ())), preferred_element_type=F32)


def _ln_stats(x):
    mu = jnp.mean(x, axis=-1, keepdims=True)
    xc = x - mu
    var = jnp.mean(xc * xc, axis=-1, keepdims=True)
    return xc * lax.rsqrt(var + LN_EPS)


def _log_sigmoid(x):
    return jnp.minimum(x, 0.0) - jnp.log1p(jnp.exp(-jnp.abs(x)))


def _sigmoid(x):
    return 1.0 / (1.0 + jnp.exp(-x))


LOG2E = 1.4426950408889634


def _neg_abs(x):
    bits = lax.bitcast_convert_type(x, jnp.int32) | jnp.int32(-2 ** 31)
    return lax.bitcast_convert_type(bits, F32)


def _div_pow2(x, n):
    assert n & (n - 1) == 0
    return jnp.right_shift(x, n.bit_length() - 1)


def _split2(x):
    hi = x.astype(BF16)
    lo = (x - hi.astype(F32)).astype(BF16)
    return hi, lo


def _split3(x):
    hi = x.astype(BF16)
    r = x - hi.astype(F32)
    mid = r.astype(BF16)
    lo = (r - mid.astype(F32)).astype(BF16)
    return hi, mid, lo


def _ada_kernel(c_ref, w_ref, b_ref, o_ref):
    c = c_ref[...]
    a = c * _sigmoid(c)
    o_ref[...] = _dot(a.astype(BF16), w_ref[...]) + b_ref[...]


def _ada(c_all, w_ada_b, b_ada):
    n, d = c_all.shape
    nout = w_ada_b.shape[1]
    blk = d
    return pl.pallas_call(
        _ada_kernel,
        grid=(nout // blk,),
        in_specs=[pl.BlockSpec((n, d), lambda j: (0, 0)),
                  pl.BlockSpec((d, blk), lambda j: (0, j)),
                  pl.BlockSpec((1, blk), lambda j: (0, j))],
        out_specs=pl.BlockSpec((n, blk), lambda j: (0, j)),
        out_shape=jax.ShapeDtypeStruct((n, nout), F32),
        compiler_params=_cparams("parallel"),
        name="ada",
    )(c_all, w_ada_b, b_ada.reshape(1, nout))


def _proj_math(x, sh1, sc1, wm_ref, wl_ref, wg2_ref, bg_ref):
    h = (_ln_stats(x) * (1.0 + sc1) + sh1).astype(BF16)
    y = _dot(h, wm_ref[...])
    g_low = _dot(h, wl_ref[...])
    gl = _dot(g_low.astype(BF16), wg2_ref[...]) + bg_ref[...]
    log_a = _log_sigmoid(gl) * (1.0 / GLA_TAU)
    return y, log_a


def _proj_prompt_kernel(x_ref, mod_ref, wm_ref, wl_ref, wg2_ref, bg_ref,
                        qT_ref, kT_ref, vT_ref, kb_ref, vTb_ref, gq_ref, gk_ref, gv_ref, gr_ref, la_ref):
    y, log_a = _proj_math(x_ref[...], mod_ref[0, 0], mod_ref[0, 1], wm_ref, wl_ref, wg2_ref, bg_ref)
    w = SB_WIDTH
    q = y[:, 0:w] * (SB_HEAD_DIM ** -0.5 * LOG2E)
    k = y[:, w:2 * w]
    v = y[:, 2 * w:3 * w]
    kb_ref[...] = k.astype(BF16)
    qT = q.T.astype(BF16)
    vT = v.T
    kT_ref[0] = k.T
    vT_ref[0] = vT
    vT = vT.astype(BF16)
    for c in range(qT_ref.shape[0]):
        qT_ref[c] = qT[:, c * SB_Q_TILE:(c + 1) * SB_Q_TILE]
    for c in range(vTb_ref.shape[0]):
        vTb_ref[c] = vT[:, c * SB_K_TILE:(c + 1) * SB_K_TILE]
    o = 3 * w
    gq_ref[...] = y[:, o:o + GLA_K_WIDTH]
    gk_ref[...] = y[:, o + GLA_K_WIDTH:o + 2 * GLA_K_WIDTH]
    o += 2 * GLA_K_WIDTH
    gv_ref[...] = y[:, o:o + GLA_V_WIDTH].astype(BF16)
    gr_ref[...] = y[:, o + GLA_V_WIDTH:o + 2 * GLA_V_WIDTH]
    la_ref[...] = log_a


def _proj_sample_kernel(x_ref, mod_ref, wm_ref, wl_ref, wg2_ref, bg_ref,
                        q_ref, k_ref, v_ref, gq_ref, gk_ref, gv_ref, gr_ref, la_ref):
    y, log_a = _proj_math(x_ref[...], mod_ref[0, 0], mod_ref[0, 1], wm_ref, wl_ref, wg2_ref, bg_ref)
    w = SB_WIDTH
    q_ref[...] = y[:, 0:w] * (SB_HEAD_DIM ** -0.5)
    k_ref[...] = y[:, w:2 * w]
    v_ref[...] = y[:, 2 * w:3 * w]
    o = 3 * w
    gq_ref[...] = y[:, o:o + GLA_K_WIDTH]
    gk_ref[...] = y[:, o + GLA_K_WIDTH:o + 2 * GLA_K_WIDTH]
    o += 2 * GLA_K_WIDTH
    gv_ref[...] = y[:, o:o + GLA_V_WIDTH]
    gr_ref[...] = y[:, o + GLA_V_WIDTH:o + 2 * GLA_V_WIDTH]
    la_ref[...] = log_a


def _const_spec(shape):
    nd = len(shape)
    return pl.BlockSpec(shape, lambda *a: (0,) * nd)


def _proj_prompt(x2d, mod, seq_len, wm, wl, wg2, bg):
    n, d = x2d.shape
    tm = min(TOKEN_TILE, seq_len)
    assert seq_len % tm == 0 and tm % SB_Q_TILE == 0
    tiles_per_seq = seq_len // tm
    row = lambda cols: pl.BlockSpec((tm, cols), lambda i: (i, 0))
    dims_major = pl.BlockSpec((1, SB_WIDTH, tm),
                              lambda i: (i // tiles_per_seq, 0, i % tiles_per_seq))
    out_shape = (
        jax.ShapeDtypeStruct((n // SB_Q_TILE, SB_WIDTH, SB_Q_TILE), BF16),
        jax.ShapeDtypeStruct((n // seq_len, SB_WIDTH, seq_len), F32),
        jax.ShapeDtypeStruct((n // seq_len, SB_WIDTH, seq_len), F32),
        jax.ShapeDtypeStruct((n, SB_WIDTH), BF16),
        jax.ShapeDtypeStruct((n // SB_K_TILE, SB_WIDTH, SB_K_TILE), BF16),
        jax.ShapeDtypeStruct((n, GLA_K_WIDTH), F32),
        jax.ShapeDtypeStruct((n, GLA_K_WIDTH), F32),
        jax.ShapeDtypeStruct((n, GLA_V_WIDTH), BF16),
        jax.ShapeDtypeStruct((n, GLA_V_WIDTH), F32),
        jax.ShapeDtypeStruct((n, GLA_K_WIDTH), F32),
    )
    out_specs = (
        pl.BlockSpec((tm // SB_Q_TILE, SB_WIDTH, SB_Q_TILE), lambda i: (i, 0, 0)),
        dims_major, dims_major, row(SB_WIDTH),
        pl.BlockSpec((tm // SB_K_TILE, SB_WIDTH, SB_K_TILE), lambda i: (i, 0, 0)),
        row(GLA_K_WIDTH), row(GLA_K_WIDTH), row(GLA_V_WIDTH), row(GLA_V_WIDTH), row(GLA_K_WIDTH),
    )
    return pl.pallas_call(
        _proj_prompt_kernel,
        grid=(n // tm,),
        in_specs=[row(d),
                  pl.BlockSpec((1, 6, 1, d), lambda i: (i // tiles_per_seq, 0, 0, 0)),
                  _const_spec(wm.shape), _const_spec(wl.shape), _const_spec(wg2.shape),
                  _const_spec(bg.shape)],
        out_specs=out_specs,
        out_shape=out_shape,
        compiler_params=_cparams("parallel"),
        name="proj_prompt",
    )(x2d, mod, wm, wl, wg2, bg)


def _proj_sample(x2d, mod, wm, wl, wg2, bg):
    n, d = x2d.shape
    full = lambda cols: pl.BlockSpec((n, cols), lambda i: (0, 0))
    widths = (SB_WIDTH, SB_WIDTH, SB_WIDTH, GLA_K_WIDTH, GLA_K_WIDTH, GLA_V_WIDTH, GLA_V_WIDTH,
              GLA_K_WIDTH)
    return pl.pallas_call(
        _proj_sample_kernel,
        grid=(1,),
        in_specs=[full(d), _const_spec(mod.shape),
                  _const_spec(wm.shape), _const_spec(wl.shape), _const_spec(wg2.shape),
                  _const_spec(bg.shape)],
        out_specs=tuple(full(c) for c in widths),
        out_shape=tuple(jax.ShapeDtypeStruct((n, c), F32) for c in widths),
        compiler_params=_cparams("arbitrary"),
        name="proj_sample",
    )(x2d, mod, wm, wl, wg2, bg)


def _sb_scan_matrix():
    k = SB_K_TILE
    s = np.arange(k)[:, None]
    j = np.arange(k)[None, :]
    upper = (j > s).astype(np.float32)
    top = np.concatenate([upper, upper], axis=1)
    ones = np.ones((16, 2 * k), np.float32)
    return jnp.asarray(np.concatenate([top, ones], axis=0), dtype=BF16)


def _sb_prompt_kernel(bias_ref, qT_ref, k_ref, vT_ref, a_ref, o_ref):
    hg = pl.program_id(1)
    i = pl.program_id(2)
    tq, tk, hd, nh = SB_Q_TILE, SB_K_TILE, SB_HEAD_DIM, SB_HEADS_PER_STEP
    n_diag = tq // tk
    scan = a_ref[...]
    head_row = lax.broadcasted_iota(jnp.int32, (2 * hd, tq), 0)
    key_in_blk = lax.broadcasted_iota(jnp.int32, (tk, tq), 0)
    q_in_tile = lax.broadcasted_iota(jnp.int32, (tk, tq), 1)

    qms, biases = [], []
    for h in range(nh):
        pair = qT_ref[0, (h // 2) * 2 * hd:(h // 2 + 1) * 2 * hd, :].astype(F32)
        own = (head_row < hd) if h % 2 == 0 else (head_row >= hd)
        qms.append(jnp.where(own, pair, 0.0).astype(BF16))
        biases.append(bias_ref[nh * hg + h] * LOG2E)

    def key_group(first_kb, state, masked):
        carries, accs = list(state[:nh]), list(state[nh:])
        chains = [(first_kb + (n_diag - 1 - u), h) for u in range(n_diag) for h in range(nh)]
        zs = []
        for kb, h in chains:
            lanes = slice((h // 2) * 2 * hd, (h // 2 + 1) * 2 * hd)
            kblk = k_ref[pl.ds(pl.multiple_of(kb * tk, tk), tk), lanes]
            zs.append(_dot(kblk, qms[h]) + biases[h])
        log_betas, halves, valids = [], [], []
        for (kb, h), z in zip(chains, zs):
            soft = jnp.log(1.0 + jnp.exp2(_neg_abs(z))) * LOG2E
            log_beta = jnp.minimum(z, 0.0) - soft
            log_keep = log_beta - z
            valid = None
            if masked:
                valid = (kb * tk + key_in_blk) < (i * tq + q_in_tile)
                log_keep = jnp.where(valid, log_keep, 0.0)
            hi, lo = _split2(log_keep)
            log_betas.append(log_beta)
            valids.append(valid)
            halves.append(jnp.concatenate([hi, lo], axis=0))
        rs = [_dot(scan, hl) for hl in halves]
        ws = []
        for (kb, h), log_beta, r, valid in zip(chains, log_betas, rs, valids):
            arg = (log_beta + r[:tk]).reshape(tk // 8, 8, tq) + carries[h]
            w = jnp.exp2(arg).reshape(tk, tq)
            if masked:
                w = jnp.where(valid, w, 0.0)
            ws.append(w.astype(BF16))
            carries[h] = carries[h] + r[tk:tk + 8]
        for (kb, h), w in zip(chains, ws):
            accs[h] = accs[h] + _dot(vT_ref[kb, h * hd:(h + 1) * hd, :], w)
        return tuple(carries) + tuple(accs)

    state = tuple([jnp.zeros((8, tq), F32)] * nh + [jnp.zeros((hd, tq), F32)] * nh)
    state = key_group(n_diag * i, state, True)
    state = lax.fori_loop(0, i, lambda t, st: key_group(n_diag * (i - 1 - t), st, False), state)

    oT = jnp.concatenate(state[nh:], axis=0)
    o_ref[...] = oT.T.astype(o_ref.dtype)


def _sb_prompt(bias, qT3, k_bf, vT3, batch, seq_len):
    n = k_bf.shape[0]
    tq, tk = SB_Q_TILE, SB_K_TILE
    assert seq_len % tq == 0
    qt = seq_len // tq
    kblocks = seq_len // tk
    width = SB_HEADS_PER_STEP * SB_HEAD_DIM
    scan = _sb_scan_matrix()
    return pl.pallas_call(
        _sb_prompt_kernel,
        grid=(batch, SB_HEADS // SB_HEADS_PER_STEP, qt),
        in_specs=[pl.BlockSpec(memory_space=pltpu.SMEM),
                  pl.BlockSpec((1, width, tq), lambda b, hg, i: (b * qt + i, hg, 0)),
                  pl.BlockSpec((seq_len, width), lambda b, hg, i: (b, hg)),
                  pl.BlockSpec((kblocks, width, tk), lambda b, hg, i: (b, hg, 0)),
                  _const_spec(scan.shape)],
        out_specs=pl.BlockSpec((tq, width), lambda b, hg, i: (b * qt + i, hg)),
        out_shape=jax.ShapeDtypeStruct((n, SB_WIDTH), BF16),
        compiler_params=_cparams("parallel", "parallel", "arbitrary"),
        name="sb_prompt",
    )(bias, qT3, k_bf, vT3, scan)


def _gla_consts():
    t = np.arange(GLA_BLOCK)
    same = (t[:, None] // GLA_CHUNK) == (t[None, :] // GLA_CHUNK)
    tri = (same & (t[None, :] <= t[:, None])).astype(np.float32)
    ones = same.astype(np.float32)
    scan = np.concatenate([np.concatenate([tri] * 3, axis=1),
                           np.concatenate([ones] * 3, axis=1)], axis=0)
    n_chunks = GLA_BLOCK // GLA_CHUNK
    ind = np.zeros((GLA_BLOCK, n_chunks * LANES), np.float32)
    for c in range(n_chunks):
        ind[c * GLA_CHUNK:(c + 1) * GLA_CHUNK, c * LANES:(c + 1) * LANES] = 1.0
    ind = np.concatenate([ind] * 3, axis=0)
    return jnp.asarray(scan, dtype=BF16), jnp.asarray(ind, dtype=BF16)


def _gla_prompt_kernel(gq_ref, gk_ref, gv_ref, la_ref, gr_ref, ng_ref, scan_ref, ind_ref,
                       og_ref, st_ref, s_scr):
    j = pl.program_id(1)
    tb, c_len = GLA_BLOCK, GLA_CHUNK
    n_chunks = tb // c_len
    dk, dv, nh = GLA_HEAD_K, GLA_HEAD_V, GLA_HEADS

    @pl.when(j == 0)
    def _():
        s_scr[...] = jnp.zeros_like(s_scr)

    la = la_ref[...]
    parts = jnp.concatenate(_split3(la), axis=0)
    bb = _dot(scan_ref[...], parts)
    b = bb[:tb]
    b_last = bb[tb:]
    laT_parts = jnp.concatenate(_split3(la.T), axis=1)
    b_last_col = _dot(laT_parts, ind_ref[...])

    q = gq_ref[...] * (dk ** -0.5)
    k = gk_ref[...]
    v = gv_ref[...]
    q_dec = q * jnp.exp(b)
    k_inv = (k * jnp.exp(-b)).astype(BF16)
    k_endT = (k * jnp.exp(b_last - b)).T

    lane_head = _div_pow2(lax.broadcasted_iota(jnp.int32, (tb, nh * dk), 1), dk)
    ti = lax.broadcasted_iota(jnp.int32, (tb, tb), 0)
    tj = lax.broadcasted_iota(jnp.int32, (tb, tb), 1)
    tj_chunk = _div_pow2(tj, c_len)
    causal = (_div_pow2(ti, c_len) == tj_chunk) & (tj <= ti)

    qm = [jnp.where(lane_head == h, q_dec, 0.0).astype(BF16) for h in range(nh)]
    o_intra = []
    for h in range(nh):
        att = jnp.where(causal, _dot_nt(qm[h], k_inv), 0.0).astype(BF16)
        o_intra.append(_dot(att, v[:, h * dv:(h + 1) * dv]))

    s_all = s_scr[...]
    norm_g = ng_ref[...]
    for c in range(n_chunks):
        rows = slice(c * c_len, (c + 1) * c_len)
        q_stack = jnp.concatenate([qm[h][rows] for h in range(nh)], axis=0)
        o_inter = _dot(q_stack, s_all.astype(BF16))
        for h in range(nh):
            o = o_intra[h][rows] + o_inter[h * c_len:(h + 1) * c_len]
            o = o * lax.rsqrt(jnp.mean(o * o, axis=-1, keepdims=True) + GN_EPS) * norm_g
            r = gr_ref[rows, h * dv:(h + 1) * dv]
            og_ref[rows, h * dv:(h + 1) * dv] = (o * (r * _sigmoid(r))).astype(og_ref.dtype)
        k_endT_c = jnp.where(tj_chunk == c, k_endT, 0.0).astype(BF16)
        u = _dot(k_endT_c, v)
        u_diag = jnp.concatenate(
            [u[h * dk:(h + 1) * dk, h * dv:(h + 1) * dv] for h in range(nh)], axis=0)
        s_all = jnp.exp(b_last_col[:, c * LANES:(c + 1) * LANES]) * s_all + u_diag
    s_scr[...] = s_all

    @pl.when(j == pl.num_programs(1) - 1)
    def _():
        st_ref[0] = s_all


def _gla_prompt(gq, gk, gv, la, gr, norm_g, batch, seq_len):
    n = gq.shape[0]
    tb = GLA_BLOCK
    assert seq_len % tb == 0
    nb = seq_len // tb
    scan, ind = _gla_consts()
    row = lambda cols: pl.BlockSpec((tb, cols), lambda b, j: (b * nb + j, 0))
    rows_s = GLA_HEADS * GLA_HEAD_K
    return pl.pallas_call(
        _gla_prompt_kernel,
        grid=(batch, nb),
        in_specs=[row(GLA_K_WIDTH), row(GLA_K_WIDTH), row(GLA_V_WIDTH), row(GLA_K_WIDTH),
                  row(GLA_V_WIDTH), _const_spec((1, GLA_HEAD_V)),
                  _const_spec(scan.shape), _const_spec(ind.shape)],
        out_specs=(row(GLA_V_WIDTH),
                   pl.BlockSpec((1, rows_s, GLA_HEAD_V), lambda b, j: (b, 0, 0))),
        out_shape=(jax.ShapeDtypeStruct((n, GLA_V_WIDTH), BF16),
                   jax.ShapeDtypeStruct((batch, rows_s, GLA_HEAD_V), F32)),
        scratch_shapes=[pltpu.VMEM((rows_s, GLA_HEAD_V), F32)],
        compiler_params=_cparams("parallel", "arbitrary"),
        name="gla_prompt",
    )(gq, gk, gv, la, gr, norm_g.reshape(1, GLA_HEAD_V), scan, ind)


def _post_kernel(alpha, x_ref, osb_ref, og_ref, mod_ref, wg_ref, wua_ref, wub_ref, wo_ref,
                 ln1g_ref, ln1b_ref, wf1_ref, wf2_ref, ln2g_ref, ln2b_ref, y_ref):
    x = x_ref[...]
    sh1, sc1, gt1, sh2, sc2, gt2 = [mod_ref[0, m] for m in range(6)]
    d = x.shape[-1]
    h = (_ln_stats(x) * (1.0 + sc1) + sh1).astype(BF16)
    merged = (_sigmoid(_dot(h, wg_ref[:, :d])) * _dot(osb_ref[...].astype(BF16), wua_ref[...])
              + _sigmoid(_dot(h, wg_ref[:, d:])) * _dot(og_ref[...].astype(BF16), wub_ref[...]))
    mixed = _dot(merged.astype(BF16), wo_ref[...])
    x1 = _ln_stats(alpha * x + gt1 * mixed) * ln1g_ref[...] + ln1b_ref[...]
    h2 = (_ln_stats(x1) * (1.0 + sc2) + sh2).astype(BF16)
    f = jnp.zeros_like(x)
    for c in range(wf1_ref.shape[1] // FF_CHUNK):
        cols = slice(c * FF_CHUNK, (c + 1) * FF_CHUNK)
        u = jnp.maximum(_dot(h2, wf1_ref[:, cols]), 0.0)
        f = f + _dot((u * u).astype(BF16), wf2_ref[cols, :])
    y_ref[...] = _ln_stats(alpha * x1 + gt2 * f) * ln2g_ref[...] + ln2b_ref[...]


def _post(x2d, o_sb, o_g, mod, seq_len, alpha, wg, wua, wub, wo, ln1g, ln1b, wf1, wf2, ln2g, ln2b):
    n, d = x2d.shape
    per_token = mod.shape[2] != 1
    tm = n if per_token else min(TOKEN_TILE, seq_len)
    tiles_per_seq = 1 if per_token else seq_len // tm
    row = lambda cols: pl.BlockSpec((tm, cols), lambda i: (i, 0))
    single = lambda a: pl.BlockSpec(a.shape, lambda i: (0,) * a.ndim, pipeline_mode=pl.Buffered(1))
    vec = lambda a: a.reshape(1, d)
    mod_spec = (_const_spec(mod.shape) if per_token else
                pl.BlockSpec((1, 6, 1, d), lambda i: (i // tiles_per_seq, 0, 0, 0)))
    return pl.pallas_call(
        functools.partial(_post_kernel, alpha),
        grid=(n // tm,),
        in_specs=[row(d), row(SB_WIDTH), row(GLA_V_WIDTH), mod_spec,
                  single(wg), single(wua), single(wub), single(wo),
                  _const_spec((1, d)), _const_spec((1, d)),
                  single(wf1), single(wf2),
                  _const_spec((1, d)), _const_spec((1, d))],
        out_specs=row(d),
        out_shape=jax.ShapeDtypeStruct((n, d), F32),
        compiler_params=_cparams("parallel"),
        name="post",
    )(x2d, o_sb, o_g, mod, wg, wua, wub, wo, vec(ln1g), vec(ln1b), wf1, wf2, vec(ln2g), vec(ln2b))


def _paged_consts():
    p, nh = PAGES_PER_STEP, SB_HEADS
    lanes = LANES
    j = np.arange(lanes)[:, None]
    s = np.arange(lanes)[None, :]
    later = (j > s).astype(np.float32)
    half = np.concatenate([later, np.ones((lanes, lanes), np.float32)], axis=1)
    lane_scan = np.concatenate([half, half], axis=0)
    r = np.arange(p * nh)
    slot, head = r // nh, r % nh
    same_head = head[:, None] == head[None, :]
    newer = same_head & (slot[None, :] < slot[:, None])
    top = np.concatenate([newer, newer], axis=1).astype(np.float32)
    bot = np.concatenate([same_head, same_head], axis=1).astype(np.float32)
    page_scan = np.concatenate([top, bot], axis=0)
    return jnp.asarray(lane_scan, dtype=BF16), jnp.asarray(page_scan, dtype=BF16)


def _paged_kernel(pt_ref, bias_ref, q_ref, lscan_ref, pscan_ref, *refs):
    p, nh, hd = PAGES_PER_STEP, SB_HEADS, SB_HEAD_DIM
    k_refs = refs[:p]
    v_refs = refs[p:2 * p]
    o_ref = refs[2 * p]
    carry_scr, acc_scr, qb_scr = refs[2 * p + 1:2 * p + 4]
    g = pl.program_id(1)
    rows = p * nh
    width = nh * hd

    @pl.when(g == 0)
    def _():
        carry_scr[...] = jnp.zeros_like(carry_scr)
        acc_scr[...] = jnp.zeros_like(acc_scr)
        qb_scr[...] = jnp.broadcast_to(q_ref[0], (width, LANES))

    sublane = lax.broadcasted_iota(jnp.int32, (nh, LANES), 0)
    zs = [jnp.zeros((nh, LANES), F32)] * p
    for h in range(nh):
        hrows = slice(h * hd, (h + 1) * hd)
        qh = qb_scr[hrows, :]
        for s in range(p):
            tot = jnp.sum(k_refs[s][0, hrows, :] * qh, axis=0, keepdims=True)
            zs[s] = jnp.where(sublane == h, tot, zs[s])
    z = jnp.concatenate(zs, axis=0)
    head_of_row = jnp.bitwise_and(lax.broadcasted_iota(jnp.int32, (rows, LANES), 0), nh - 1)
    bias = jnp.zeros((rows, LANES), F32)
    for h in range(nh):
        bias = jnp.where(head_of_row == h, bias_ref[h], bias)
    z = z + bias
    soft = jnp.log1p(jnp.exp(-jnp.abs(z)))
    log_beta = jnp.minimum(z, 0.0) - soft
    log_keep = log_beta - z
    hi, lo = _split2(log_keep)
    r1 = _dot(jnp.concatenate([hi, lo], axis=1), lscan_ref[...])
    rest_in = r1[:, :LANES]
    row_tot = r1[:, LANES:]
    thi, tlo = _split2(row_tot)
    r2 = _dot(pscan_ref[...], jnp.concatenate([thi, tlo], axis=0))
    carry = carry_scr[...]
    w = jnp.exp(log_beta + rest_in + r2[:rows] + carry)
    carry_scr[...] = carry + r2[rows:]

    for h in range(nh):
        hrows = slice(h * hd, (h + 1) * hd)
        acc = acc_scr[hrows, :]
        for s in range(p):
            acc = acc + v_refs[s][0, hrows, :] * w[s * nh + h:s * nh + h + 1, :]
        acc_scr[hrows, :] = acc

    @pl.when(g == pl.num_programs(1) - 1)
    def _():
        o_ref[0] = jnp.sum(acc_scr[...], axis=-1, keepdims=True)


def _paged_attention(page_table, bias, q, cache_kT, cache_vT):
    nseq, n_pages = page_table.shape
    p = PAGES_PER_STEP
    width = SB_WIDTH
    assert n_pages % p == 0 and cache_kT.shape[1:] == (width, LANES)
    groups = n_pages // p
    lscan, pscan = _paged_consts()

    def page_spec(s):
        def imap(b, g, pt):
            return (pt[b, n_pages - 1 - (g * p + s)], 0, 0)
        return pl.BlockSpec((1, width, LANES), imap)

    smem = pl.BlockSpec(memory_space=pltpu.SMEM)
    grid_spec = pltpu.PrefetchScalarGridSpec(
        num_scalar_prefetch=1,
        grid=(nseq, groups),
        in_specs=[smem,
                  pl.BlockSpec((1, width, 1), lambda b, g, pt: (b, 0, 0)),
                  pl.BlockSpec(lscan.shape, lambda b, g, pt: (0, 0)),
                  pl.BlockSpec(pscan.shape, lambda b, g, pt: (0, 0))]
                 + [page_spec(s) for s in range(p)] + [page_spec(s) for s in range(p)],
        out_specs=pl.BlockSpec((1, width, 1), lambda b, g, pt: (b, 0, 0)),
        scratch_shapes=[pltpu.VMEM((p * SB_HEADS, LANES), F32),
                        pltpu.VMEM((width, LANES), F32),
                        pltpu.VMEM((width, LANES), F32)],
    )
    out = pl.pallas_call(
        _paged_kernel,
        grid_spec=grid_spec,
        out_shape=jax.ShapeDtypeStruct((nseq, width, 1), F32),
        compiler_params=_cparams("parallel", "arbitrary"),
        name="sb_paged",
    )(page_table, bias, q.reshape(nseq, width, 1), lscan, pscan,
      *([cache_kT] * p), *([cache_vT] * p))
    return out.reshape(nseq, width)


def _gla_sample_kernel(gq_ref, gk_ref, la_ref, gv_ref, gr_ref, ng_ref, s0_ref, og_ref, st_ref):
    nh, dk, dv = GLA_HEADS, GLA_HEAD_K, GLA_HEAD_V
    la = la_ref[0]
    q_dec = gq_ref[0] * (dk ** -0.5) * jnp.exp(la)
    k_inv = gk_ref[0] * jnp.exp(-la)
    k_end = gk_ref[0]
    s0 = s0_ref[0]
    v8 = gv_ref[0]
    qk = q_dec * k_inv
    qs = q_dec * s0
    outs = []
    for h in range(nh):
        rows = slice(h * dk, (h + 1) * dk)
        att = jnp.sum(qk[rows], axis=0, keepdims=True)
        outs.append(att * v8[h:h + 1] + jnp.sum(qs[rows], axis=0, keepdims=True))
    o = jnp.concatenate(outs + [jnp.zeros((8 - nh, dv), F32)], axis=0)
    o = o * lax.rsqrt(jnp.mean(o * o, axis=-1, keepdims=True) + GN_EPS) * ng_ref[...]
    r = gr_ref[0]
    og_ref[0] = o * (r * _sigmoid(r))
    v_rows = jnp.concatenate(
        [jnp.broadcast_to(v8[h:h + 1], (dk, dv)) for h in range(nh)], axis=0)
    st_ref[0] = jnp.exp(la) * s0 + k_end * v_rows


def _gla_sample(gq, gk, gv, la, gr, norm_g, s0):
    n = gq.shape[0]
    nh, dk, dv = GLA_HEADS, GLA_HEAD_K, GLA_HEAD_V
    kw = nh * dk
    pad_heads = lambda a: jnp.pad(a.reshape(n, nh, dv), ((0, 0), (0, 8 - nh), (0, 0)))
    colspec = pl.BlockSpec((1, kw, 1), lambda b: (b, 0, 0))
    headspec = pl.BlockSpec((1, 8, dv), lambda b: (b, 0, 0))
    statespec = pl.BlockSpec((1, kw, dv), lambda b: (b, 0, 0))
    og, st = pl.pallas_call(
        _gla_sample_kernel,
        grid=(n,),
        in_specs=[colspec, colspec, colspec, headspec, headspec, _const_spec((1, dv)), statespec],
        out_specs=(headspec, statespec),
        out_shape=(jax.ShapeDtypeStruct((n, 8, dv), F32),
                   jax.ShapeDtypeStruct((n, kw, dv), F32)),
        compiler_params=_cparams("parallel"),
        name="gla_sample",
    )(gq.reshape(n, kw, 1), gk.reshape(n, kw, 1), la.reshape(n, kw, 1),
      pad_heads(gv), pad_heads(gr), norm_g.reshape(1, dv), s0)
    return og[:, :nh].reshape(n, nh * dv), st


def kernel(x_prompt, x_sample, cache_k, cache_v, state_gla, page_table, c_prompt, c_sample,
           w_ada, b_ada, w_in, sb_bias, w_gla_g2, b_gla_g, gla_norm_g, w_up_a, w_up_b, w_o,
           ln1_g, ln1_b, w_ff1, w_ff2, ln2_g, ln2_b):
    depth = w_ada.shape[0]
    pb, seq_len, d = x_prompt.shape
    sb_n = x_sample.shape[0]
    assert x_sample.shape[1] == 1
    alpha = (2.0 * depth) ** 0.25
    n_phys, page_size = cache_k.shape[1], cache_k.shape[2]

    yp = x_prompt.reshape(pb * seq_len, d)
    ys = x_sample.reshape(sb_n, d)
    c_all = jnp.concatenate([c_prompt, c_sample], axis=0)
    main_w = 3 * SB_WIDTH + 2 * GLA_K_WIDTH + 2 * GLA_V_WIDTH
    gate_off = main_w + GLA_GATE_RANK

    outs = [[] for _ in range(6)]
    for l in range(depth):
        ada = _ada(c_all, w_ada[l].astype(BF16), b_ada[l])
        mod_p = ada[:pb].reshape(pb, 6, 1, d)
        mod_s = ada[pb:].reshape(sb_n, 6, d).transpose(1, 0, 2).reshape(1, 6, sb_n, d)

        w_in_l = w_in[l]
        wm = w_in_l[:, :main_w].astype(BF16)
        wl = jnp.pad(w_in_l[:, main_w:gate_off], ((0, 0), (0, LANES - GLA_GATE_RANK))).astype(BF16)
        wg = w_in_l[:, gate_off:].astype(BF16)
        wg2 = jnp.pad(w_gla_g2[l], ((0, LANES - GLA_GATE_RANK), (0, 0))).astype(BF16)
        bg = b_gla_g[l].reshape(1, GLA_K_WIDTH)
        post_w = (wg, w_up_a[l].astype(BF16), w_up_b[l].astype(BF16), w_o[l].astype(BF16),
                  ln1_g[l], ln1_b[l], w_ff1[l].astype(BF16), w_ff2[l].astype(BF16),
                  ln2_g[l], ln2_b[l])

        qT3, k_p, v_p, k_bf, vT3, gq, gk, gv, gr, la = _proj_prompt(yp, mod_p, seq_len, wm, wl, wg2, bg)
        o_sb = _sb_prompt(sb_bias[l], qT3, k_bf, vT3, pb, seq_len)
        o_g, st_p = _gla_prompt(gq, gk, gv, la, gr, gla_norm_g[l], pb, seq_len)
        yp = _post(yp, o_sb, o_g, mod_p, seq_len, alpha, *post_w)

        q_s, k_s, v_s, gq_s, gk_s, gv_s, gr_s, la_s = _proj_sample(ys, mod_s, wm, wl, wg2, bg)
        page_major = lambda c: jnp.transpose(c, (0, 2, 3, 1)).reshape(n_phys, SB_WIDTH, page_size)
        o_sb_s = _paged_attention(page_table, sb_bias[l], q_s,
                                  page_major(cache_k[l]), page_major(cache_v[l]))
        s0 = state_gla[l].reshape(sb_n, GLA_HEADS * GLA_HEAD_K, GLA_HEAD_V)
        o_g_s, st_s = _gla_sample(gq_s, gk_s, gv_s, la_s, gr_s, gla_norm_g[l], s0)
        ys = _post(ys, o_sb_s, o_g_s, mod_s, 1, alpha, *post_w)

        rows_major = lambda t: jnp.transpose(
            t.reshape(pb, SB_HEADS, SB_HEAD_DIM, seq_len), (0, 3, 1, 2))
        outs[0].append(rows_major(k_p))
        outs[1].append(rows_major(v_p))
        outs[2].append(st_p.reshape(pb, GLA_HEADS, GLA_HEAD_K, GLA_HEAD_V))
        outs[3].append(k_s.reshape(sb_n, 1, SB_HEADS, SB_HEAD_DIM))
        outs[4].append(v_s.reshape(sb_n, 1, SB_HEADS, SB_HEAD_DIM))
        outs[5].append(st_s.reshape(sb_n, GLA_HEADS, GLA_HEAD_K, GLA_HEAD_V))

    stacked = [jnp.stack(o) for o in outs]
    return (yp.reshape(pb, seq_len, d), ys.reshape(sb_n, 1, d),
            stacked[0], stacked[1], stacked[2].astype(state_gla.dtype),
            stacked[3], stacked[4], stacked[5].astype(state_gla.dtype))
```

```python
import functools

import numpy as np
import jax
import jax.numpy as jnp
from jax import lax
from jax.experimental import pallas as pl
from jax.experimental.pallas import tpu as pltpu

F32 = jnp.float32
BF16 = jnp.bfloat16

SB_HEADS = 8
SB_HEAD_DIM = 64
SB_WIDTH = SB_HEADS * SB_HEAD_DIM
GLA_HEADS = 4
GLA_HEAD_K = 64
GLA_HEAD_V = 128
GLA_K_WIDTH = GLA_HEADS * GLA_HEAD_K
GLA_V_WIDTH = GLA_HEADS * GLA_HEAD_V
GLA_GATE_RANK = 16
GLA_TAU = 16.0
GLA_CHUNK = 64
LN_EPS = 1e-5
GN_EPS = 1e-6

LANES = 128
VMEM_LIMIT_BYTES = 56 * 1024 * 1024

TOKEN_TILE = 512
SB_Q_TILE = 256
SB_K_TILE = 128
SB_HEADS_PER_STEP = 8
GLA_BLOCK = 256
PAGED_PAGES = 8
FF_CHUNK = 1024


def _cparams(*sem):
    return pltpu.CompilerParams(dimension_semantics=sem, vmem_limit_bytes=VMEM_LIMIT_BYTES)


def _dot(a, b):
    return jnp.dot(a, b, preferred_element_type=F32)


def _dot_nt(a, b):
    return lax.dot_general(a, b, (((1,), (1,)), ((), ())), preferred_element_type=F32)


def _ln_stats(x):
    mu = jnp.mean(x, axis=-1, keepdims=True)
    xc = x - mu
    var = jnp.mean(xc * xc, axis=-1, keepdims=True)
    return xc * lax.rsqrt(var + LN_EPS)


def _log_sigmoid(x):
    return jnp.minimum(x, 0.0) - jnp.log1p(jnp.exp(-jnp.abs(x)))


def _sigmoid(x):
    return 1.0 / (1.0 + jnp.exp(-x))


LOG2E = 1.4426950408889634


def _neg_abs(x):
    bits = lax.bitcast_convert_type(x, jnp.int32) | jnp.int32(-2 ** 31)
    return lax.bitcast_convert_type(bits, F32)


def _div_pow2(x, n):
    assert n & (n - 1) == 0
    return jnp.right_shift(x, n.bit_length() - 1)


def _split2(x):
    hi = x.astype(BF16)
    lo = (x - hi.astype(F32)).astype(BF16)
    return hi, lo


def _split3(x):
    hi = x.astype(BF16)
    r = x - hi.astype(F32)
    mid = r.astype(BF16)
    lo = (r - mid.astype(F32)).astype(BF16)
    return hi, mid, lo


def _ada_kernel(c_ref, w_ref, b_ref, o_ref):
    c = c_ref[...]
    a = c * _sigmoid(c)
    o_ref[...] = _dot(a.astype(BF16), w_ref[...]) + b_ref[...]


def _ada(c_all, w_ada_b, b_ada):
    n, d = c_all.shape
    nout = w_ada_b.shape[1]
    blk = d
    return pl.pallas_call(
        _ada_kernel,
        grid=(nout // blk,),
        in_specs=[pl.BlockSpec((n, d), lambda j: (0, 0)),
                  pl.BlockSpec((d, blk), lambda j: (0, j)),
                  pl.BlockSpec((1, blk), lambda j: (0, j))],
        out_specs=pl.BlockSpec((n, blk), lambda j: (0, j)),
        out_shape=jax.ShapeDtypeStruct((n, nout), F32),
        compiler_params=_cparams("parallel"),
        name="ada",
    )(c_all, w_ada_b, b_ada.reshape(1, nout))


def _proj_math(x, sh1, sc1, wm_ref, wl_ref, wg2_ref, bg_ref):
    h = (_ln_stats(x) * (1.0 + sc1) + sh1).astype(BF16)
    y = _dot(h, wm_ref[...])
    g_low = _dot(h, wl_ref[...])
    gl = _dot(g_low.astype(BF16), wg2_ref[...]) + bg_ref[...]
    log_a = _log_sigmoid(gl) * (1.0 / GLA_TAU)
    return y, log_a


def _proj_prompt_kernel(x_ref, mod_ref, wm_ref, wl_ref, wg2_ref, bg_ref,
                        qT_ref, kT_ref, vT_ref, kb_ref, vTb_ref, gq_ref, gk_ref, gv_ref, gr_ref, la_ref):
    y, log_a = _proj_math(x_ref[...], mod_ref[0, 0], mod_ref[0, 1], wm_ref, wl_ref, wg2_ref, bg_ref)
    w = SB_WIDTH
    q = y[:, 0:w] * (SB_HEAD_DIM ** -0.5 * LOG2E)
    k = y[:, w:2 * w]
    v = y[:, 2 * w:3 * w]
    kb_ref[...] = k.astype(BF16)
    qT = q.T.astype(BF16)
    vT = v.T
    kT_ref[0] = k.T
    vT_ref[0] = vT
    vT = vT.astype(BF16)
    for c in range(qT_ref.shape[0]):
        qT_ref[c] = qT[:, c * SB_Q_TILE:(c + 1) * SB_Q_TILE]
    for c in range(vTb_ref.shape[0]):
        vTb_ref[c] = vT[:, c * SB_K_TILE:(c + 1) * SB_K_TILE]
    o = 3 * w
    gq_ref[...] = y[:, o:o + GLA_K_WIDTH]
    gk_ref[...] = y[:, o + GLA_K_WIDTH:o + 2 * GLA_K_WIDTH]
    o += 2 * GLA_K_WIDTH
    gv_ref[...] = y[:, o:o + GLA_V_WIDTH].astype(BF16)
    gr_ref[...] = y[:, o + GLA_V_WIDTH:o + 2 * GLA_V_WIDTH]
    la_ref[...] = log_a


def _proj_sample_kernel(x_ref, mod_ref, wm_ref, wl_ref, wg2_ref, bg_ref,
                        q_ref, k_ref, v_ref, gq_ref, gk_ref, gv_ref, gr_ref, la_ref):
    y, log_a = _proj_math(x_ref[...], mod_ref[0, 0], mod_ref[0, 1], wm_ref, wl_ref, wg2_ref, bg_ref)
    w = SB_WIDTH
    q_ref[...] = y[:, 0:w] * (SB_HEAD_DIM ** -0.5)
    k_ref[...] = y[:, w:2 * w]
    v_ref[...] = y[:, 2 * w:3 * w]
    o = 3 * w
    gq_ref[...] = y[:, o:o + GLA_K_WIDTH]
    gk_ref[...] = y[:, o + GLA_K_WIDTH:o + 2 * GLA_K_WIDTH]
    o += 2 * GLA_K_WIDTH
    gv_ref[...] = y[:, o:o + GLA_V_WIDTH]
    gr_ref[...] = y[:, o + GLA_V_WIDTH:o + 2 * GLA_V_WIDTH]
    la_ref[...] = log_a


def _const_spec(shape):
    nd = len(shape)
    return pl.BlockSpec(shape, lambda *a: (0,) * nd)


def _proj_prompt(x2d, mod, seq_len, wm, wl, wg2, bg):
    n, d = x2d.shape
    tm = min(TOKEN_TILE, seq_len)
    assert seq_len % tm == 0 and tm % SB_Q_TILE == 0
    tiles_per_seq = seq_len // tm
    row = lambda cols: pl.BlockSpec((tm, cols), lambda i: (i, 0))
    dims_major = pl.BlockSpec((1, SB_WIDTH, tm),
                              lambda i: (i // tiles_per_seq, 0, i % tiles_per_seq))
    out_shape = (
        jax.ShapeDtypeStruct((n // SB_Q_TILE, SB_WIDTH, SB_Q_TILE), BF16),
        jax.ShapeDtypeStruct((n // seq_len, SB_WIDTH, seq_len), F32),
        jax.ShapeDtypeStruct((n // seq_len, SB_WIDTH, seq_len), F32),
        jax.ShapeDtypeStruct((n, SB_WIDTH), BF16),
        jax.ShapeDtypeStruct((n // SB_K_TILE, SB_WIDTH, SB_K_TILE), BF16),
        jax.ShapeDtypeStruct((n, GLA_K_WIDTH), F32),
        jax.ShapeDtypeStruct((n, GLA_K_WIDTH), F32),
        jax.ShapeDtypeStruct((n, GLA_V_WIDTH), BF16),
        jax.ShapeDtypeStruct((n, GLA_V_WIDTH), F32),
        jax.ShapeDtypeStruct((n, GLA_K_WIDTH), F32),
    )
    out_specs = (
        pl.BlockSpec((tm // SB_Q_TILE, SB_WIDTH, SB_Q_TILE), lambda i: (i, 0, 0)),
        dims_major, dims_major, row(SB_WIDTH),
        pl.BlockSpec((tm // SB_K_TILE, SB_WIDTH, SB_K_TILE), lambda i: (i, 0, 0)),
        row(GLA_K_WIDTH), row(GLA_K_WIDTH), row(GLA_V_WIDTH), row(GLA_V_WIDTH), row(GLA_K_WIDTH),
    )
    return pl.pallas_call(
        _proj_prompt_kernel,
        grid=(n // tm,),
        in_specs=[row(d),
                  pl.BlockSpec((1, 6, 1, d), lambda i: (i // tiles_per_seq, 0, 0, 0)),
                  _const_spec(wm.shape), _const_spec(wl.shape), _const_spec(wg2.shape),
                  _const_spec(bg.shape)],
        out_specs=out_specs,
        out_shape=out_shape,
        compiler_params=_cparams("parallel"),
        name="proj_prompt",
    )(x2d, mod, wm, wl, wg2, bg)


def _proj_sample(x2d, mod, wm, wl, wg2, bg):
    n, d = x2d.shape
    full = lambda cols: pl.BlockSpec((n, cols), lambda i: (0, 0))
    widths = (SB_WIDTH, SB_WIDTH, SB_WIDTH, GLA_K_WIDTH, GLA_K_WIDTH, GLA_V_WIDTH, GLA_V_WIDTH,
              GLA_K_WIDTH)
    return pl.pallas_call(
        _proj_sample_kernel,
        grid=(1,),
        in_specs=[full(d), _const_spec(mod.shape),
                  _const_spec(wm.shape), _const_spec(wl.shape), _const_spec(wg2.shape),
                  _const_spec(bg.shape)],
        out_specs=tuple(full(c) for c in widths),
        out_shape=tuple(jax.ShapeDtypeStruct((n, c), F32) for c in widths),
        compiler_params=_cparams("arbitrary"),
        name="proj_sample",
    )(x2d, mod, wm, wl, wg2, bg)


def _sb_scan_matrix():
    k = SB_K_TILE
    s = np.arange(k)[:, None]
    j = np.arange(k)[None, :]
    upper = (j > s).astype(np.float32)
    top = np.concatenate([upper, upper], axis=1)
    ones = np.ones((16, 2 * k), np.float32)
    return jnp.asarray(np.concatenate([top, ones], axis=0), dtype=BF16)


def _sb_prompt_kernel(bias_ref, qT_ref, k_ref, vT_ref, a_ref, o_ref):
    hg = pl.program_id(1)
    i = pl.program_id(2)
    tq, tk, hd, nh = SB_Q_TILE, SB_K_TILE, SB_HEAD_DIM, SB_HEADS_PER_STEP
    n_diag = tq // tk
    scan = a_ref[...]
    head_row = lax.broadcasted_iota(jnp.int32, (2 * hd, tq), 0)
    key_in_blk = lax.broadcasted_iota(jnp.int32, (tk, tq), 0)
    q_in_tile = lax.broadcasted_iota(jnp.int32, (tk, tq), 1)

    qms, biases = [], []
    for h in range(nh):
        pair = qT_ref[0, (h // 2) * 2 * hd:(h // 2 + 1) * 2 * hd, :].astype(F32)
        own = (head_row < hd) if h % 2 == 0 else (head_row >= hd)
        qms.append(jnp.where(own, pair, 0.0).astype(BF16))
        biases.append(bias_ref[nh * hg + h] * LOG2E)

    def key_group(first_kb, state, masked):
        carries, accs = list(state[:nh]), list(state[nh:])
        chains = [(first_kb + (n_diag - 1 - u), h) for u in range(n_diag) for h in range(nh)]
        zs = []
        for kb, h in chains:
            lanes = slice((h // 2) * 2 * hd, (h // 2 + 1) * 2 * hd)
            kblk = k_ref[pl.ds(pl.multiple_of(kb * tk, tk), tk), lanes]
            zs.append(_dot(kblk, qms[h]) + biases[h])
        log_betas, halves, valids = [], [], []
        for (kb, h), z in zip(chains, zs):
            soft = jnp.log(1.0 + jnp.exp2(_neg_abs(z))) * LOG2E
            log_beta = jnp.minimum(z, 0.0) - soft
            log_keep = log_beta - z
            valid = None
            if masked:
                valid = (kb * tk + key_in_blk) < (i * tq + q_in_tile)
                log_keep = jnp.where(valid, log_keep, 0.0)
            hi, lo = _split2(log_keep)
            log_betas.append(log_beta)
            valids.append(valid)
            halves.append(jnp.concatenate([hi, lo], axis=0))
        rs = [_dot(scan, hl) for hl in halves]
        ws = []
        for (kb, h), log_beta, r, valid in zip(chains, log_betas, rs, valids):
            arg = (log_beta + r[:tk]).reshape(tk // 8, 8, tq) + carries[h]
            w = jnp.exp2(arg).reshape(tk, tq)
            if masked:
                w = jnp.where(valid, w, 0.0)
            ws.append(w.astype(BF16))
            carries[h] = carries[h] + r[tk:tk + 8]
        for (kb, h), w in zip(chains, ws):
            accs[h] = accs[h] + _dot(vT_ref[kb, h * hd:(h + 1) * hd, :], w)
        return tuple(carries) + tuple(accs)

    state = tuple([jnp.zeros((8, tq), F32)] * nh + [jnp.zeros((hd, tq), F32)] * nh)
    state = key_group(n_diag * i, state, True)
    state = lax.fori_loop(0, i, lambda t, st: key_group(n_diag * (i - 1 - t), st, False), state)

    oT = jnp.concatenate(state[nh:], axis=0)
    o_ref[...] = oT.T.astype(o_ref.dtype)


def _sb_prompt(bias, qT3, k_bf, vT3, batch, seq_len):
    n = k_bf.shape[0]
    tq, tk = SB_Q_TILE, SB_K_TILE
    assert seq_len % tq == 0
    qt = seq_len // tq
    kblocks = seq_len // tk
    width = SB_HEADS_PER_STEP * SB_HEAD_DIM
    scan = _sb_scan_matrix()
    return pl.pallas_call(
        _sb_prompt_kernel,
        grid=(batch, SB_HEADS // SB_HEADS_PER_STEP, qt),
        in_specs=[pl.BlockSpec(memory_space=pltpu.SMEM),
                  pl.BlockSpec((1, width, tq), lambda b, hg, i: (b * qt + i, hg, 0)),
                  pl.BlockSpec((seq_len, width), lambda b, hg, i: (b, hg)),
                  pl.BlockSpec((kblocks, width, tk), lambda b, hg, i: (b, hg, 0)),
                  _const_spec(scan.shape)],
        out_specs=pl.BlockSpec((tq, width), lambda b, hg, i: (b * qt + i, hg)),
        out_shape=jax.ShapeDtypeStruct((n, SB_WIDTH), BF16),
        compiler_params=_cparams("parallel", "parallel", "arbitrary"),
        name="sb_prompt",
    )(bias, qT3, k_bf, vT3, scan)


def _gla_consts():
    t = np.arange(GLA_BLOCK)
    same = (t[:, None] // GLA_CHUNK) == (t[None, :] // GLA_CHUNK)
    tri = (same & (t[None, :] <= t[:, None])).astype(np.float32)
    ones = same.astype(np.float32)
    scan = np.concatenate([np.concatenate([tri] * 3, axis=1),
                           np.concatenate([ones] * 3, axis=1)], axis=0)
    n_chunks = GLA_BLOCK // GLA_CHUNK
    ind = np.zeros((GLA_BLOCK, n_chunks * LANES), np.float32)
    for c in range(n_chunks):
        ind[c * GLA_CHUNK:(c + 1) * GLA_CHUNK, c * LANES:(c + 1) * LANES] = 1.0
    ind = np.concatenate([ind] * 3, axis=0)
    return jnp.asarray(scan, dtype=BF16), jnp.asarray(ind, dtype=BF16)


def _gla_prompt_kernel(gq_ref, gk_ref, gv_ref, la_ref, gr_ref, ng_ref, scan_ref, ind_ref,
                       og_ref, st_ref, s_scr):
    j = pl.program_id(1)
    tb, c_len = GLA_BLOCK, GLA_CHUNK
    n_chunks = tb // c_len
    dk, dv, nh = GLA_HEAD_K, GLA_HEAD_V, GLA_HEADS

    @pl.when(j == 0)
    def _():
        s_scr[...] = jnp.zeros_like(s_scr)

    la = la_ref[...]
    parts = jnp.concatenate(_split3(la), axis=0)
    bb = _dot(scan_ref[...], parts)
    b = bb[:tb]
    b_last = bb[tb:]
    laT_parts = jnp.concatenate(_split3(la.T), axis=1)
    b_last_col = _dot(laT_parts, ind_ref[...])

    q = gq_ref[...] * (dk ** -0.5)
    k = gk_ref[...]
    v = gv_ref[...]
    q_dec = q * jnp.exp(b)
    k_inv = (k * jnp.exp(-b)).astype(BF16)
    k_endT = (k * jnp.exp(b_last - b)).T

    lane_head = _div_pow2(lax.broadcasted_iota(jnp.int32, (tb, nh * dk), 1), dk)
    ti = lax.broadcasted_iota(jnp.int32, (tb, tb), 0)
    tj = lax.broadcasted_iota(jnp.int32, (tb, tb), 1)
    tj_chunk = _div_pow2(tj, c_len)
    causal = (_div_pow2(ti, c_len) == tj_chunk) & (tj <= ti)

    qm = [jnp.where(lane_head == h, q_dec, 0.0).astype(BF16) for h in range(nh)]
    scores = [_dot_nt(qm[h], k_inv) for h in range(nh)]
    us = []
    for c in range(n_chunks):
        k_endT_c = jnp.where(tj_chunk == c, k_endT, 0.0).astype(BF16)
        us.append(_dot(k_endT_c, v))
    o_intra = [_dot(jnp.where(causal, scores[h], 0.0).astype(BF16), v[:, h * dv:(h + 1) * dv])
               for h in range(nh)]

    states = [s_scr[...]]
    for c in range(n_chunks):
        u_diag = jnp.concatenate(
            [us[c][h * dk:(h + 1) * dk, h * dv:(h + 1) * dv] for h in range(nh)], axis=0)
        states.append(jnp.exp(b_last_col[:, c * LANES:(c + 1) * LANES]) * states[c] + u_diag)
    o_inter = []
    for c in range(n_chunks):
        rows = slice(c * c_len, (c + 1) * c_len)
        q_stack = jnp.concatenate([qm[h][rows] for h in range(nh)], axis=0)
        o_inter.append(_dot(q_stack, states[c].astype(BF16)))

    norm_g = ng_ref[...]
    for c in range(n_chunks):
        rows = slice(c * c_len, (c + 1) * c_len)
        for h in range(nh):
            o = o_intra[h][rows] + o_inter[c][h * c_len:(h + 1) * c_len]
            o = o * lax.rsqrt(jnp.mean(o * o, axis=-1, keepdims=True) + GN_EPS) * norm_g
            r = gr_ref[rows, h * dv:(h + 1) * dv]
            og_ref[rows, h * dv:(h + 1) * dv] = (o * (r * _sigmoid(r))).astype(og_ref.dtype)
    s_all = states[n_chunks]
    s_scr[...] = s_all

    @pl.when(j == pl.num_programs(1) - 1)
    def _():
        st_ref[0] = s_all


def _gla_prompt(gq, gk, gv, la, gr, norm_g, batch, seq_len):
    n = gq.shape[0]
    tb = GLA_BLOCK
    assert seq_len % tb == 0
    nb = seq_len // tb
    scan, ind = _gla_consts()
    row = lambda cols: pl.BlockSpec((tb, cols), lambda b, j: (b * nb + j, 0))
    rows_s = GLA_HEADS * GLA_HEAD_K
    return pl.pallas_call(
        _gla_prompt_kernel,
        grid=(batch, nb),
        in_specs=[row(GLA_K_WIDTH), row(GLA_K_WIDTH), row(GLA_V_WIDTH), row(GLA_K_WIDTH),
                  row(GLA_V_WIDTH), _const_spec((1, GLA_HEAD_V)),
                  _const_spec(scan.shape), _const_spec(ind.shape)],
        out_specs=(row(GLA_V_WIDTH),
                   pl.BlockSpec((1, rows_s, GLA_HEAD_V), lambda b, j: (b, 0, 0))),
        out_shape=(jax.ShapeDtypeStruct((n, GLA_V_WIDTH), BF16),
                   jax.ShapeDtypeStruct((batch, rows_s, GLA_HEAD_V), F32)),
        scratch_shapes=[pltpu.VMEM((rows_s, GLA_HEAD_V), F32)],
        compiler_params=_cparams("parallel", "arbitrary"),
        name="gla_prompt",
    )(gq, gk, gv, la, gr, norm_g.reshape(1, GLA_HEAD_V), scan, ind)


POST_SEGMENTS = 8


def _post_math(alpha, x_ref, osb_ref, og_ref, mod_ref, wg_ref, wua_ref, wub_ref, wo_ref,
               ln1g_ref, ln1b_ref, wf1_ref, wf2_ref, ln2g_ref, ln2b_ref, y_ref, segment):
    x = x_ref[...]
    sh1, sc1, gt1, sh2, sc2, gt2 = [mod_ref[0, m] for m in range(6)]
    d = x.shape[-1]
    n_ff = wf1_ref.shape[1] // FF_CHUNK
    assert POST_SEGMENTS == 4 + n_ff

    def seg0():
        h = (_ln_stats(x) * (1.0 + sc1) + sh1).astype(BF16)
        return h, _sigmoid(_dot(h, wg_ref[:, :d]))

    h, gate_a = segment(0, seg0)
    part_a, gate_b = segment(1, lambda: (gate_a * _dot(osb_ref[...].astype(BF16), wua_ref[...]),
                                         _sigmoid(_dot(h, wg_ref[:, d:]))))
    merged = segment(2, lambda: (part_a + gate_b * _dot(og_ref[...].astype(BF16), wub_ref[...])
                                 ).astype(BF16))

    def seg3():
        x1 = _ln_stats(alpha * x + gt1 * _dot(merged, wo_ref[...])) * ln1g_ref[...] + ln1b_ref[...]
        return x1, (_ln_stats(x1) * (1.0 + sc2) + sh2).astype(BF16)

    x1, h2 = segment(3, seg3)
    f = jnp.zeros_like(x)
    for c in range(n_ff):
        cols = slice(c * FF_CHUNK, (c + 1) * FF_CHUNK)

        def ff(f=f, cols=cols, last=(c == n_ff - 1)):
            u = jnp.maximum(_dot(h2, wf1_ref[:, cols]), 0.0)
            f_new = f + _dot((u * u).astype(BF16), wf2_ref[cols, :])
            if last:
                y_ref[...] = _ln_stats(alpha * x1 + gt2 * f_new) * ln2g_ref[...] + ln2b_ref[...]
            return f_new

        f = segment(4 + c, ff)


def _post_kernel(alpha, *refs):
    _post_math(alpha, *refs, segment=lambda j, fn: fn())


def _post(x2d, o_sb, o_g, mod, seq_len, alpha, wg, wua, wub, wo, ln1g, ln1b, wf1, wf2, ln2g, ln2b):
    n, d = x2d.shape
    per_token = mod.shape[2] != 1
    tm = n if per_token else min(TOKEN_TILE, seq_len)
    tiles_per_seq = 1 if per_token else seq_len // tm
    row = lambda cols: pl.BlockSpec((tm, cols), lambda i: (i, 0))
    single = lambda a: pl.BlockSpec(a.shape, lambda i: (0,) * a.ndim, pipeline_mode=pl.Buffered(1))
    vec = lambda a: a.reshape(1, d)
    mod_spec = (_const_spec(mod.shape) if per_token else
                pl.BlockSpec((1, 6, 1, d), lambda i: (i // tiles_per_seq, 0, 0, 0)))
    return pl.pallas_call(
        functools.partial(_post_kernel, alpha),
        grid=(n // tm,),
        in_specs=[row(d), row(SB_WIDTH), row(GLA_V_WIDTH), mod_spec,
                  single(wg), single(wua), single(wub), single(wo),
                  _const_spec((1, d)), _const_spec((1, d)),
                  single(wf1), single(wf2),
                  _const_spec((1, d)), _const_spec((1, d))],
        out_specs=row(d),
        out_shape=jax.ShapeDtypeStruct((n, d), F32),
        compiler_params=_cparams("parallel"),
        name="post",
    )(x2d, o_sb, o_g, mod, wg, wua, wub, wo, vec(ln1g), vec(ln1b), wf1, wf2, vec(ln2g), vec(ln2b))


def _lane_scan_matrix():
    j = np.arange(LANES)[:, None]
    s = np.arange(LANES)[None, :]
    later = (j > s).astype(np.float32)
    half = np.concatenate([later, np.ones((LANES, LANES), np.float32)], axis=1)
    return jnp.asarray(np.concatenate([half, half], axis=0), dtype=BF16)


def _paged_logits(k_pages, qb_scr, bias_ref, lb_scr, hl_scr):
    p, nh, hd = PAGED_PAGES, SB_HEADS, SB_HEAD_DIM
    sublane = lax.broadcasted_iota(jnp.int32, (nh, LANES), 0)
    zs = [jnp.zeros((nh, LANES), F32)] * p
    bias = jnp.zeros((nh, LANES), F32)
    for h in range(nh):
        hrows = slice(h * hd, (h + 1) * hd)
        qh = qb_scr[hrows, :]
        bias = jnp.where(sublane == h, bias_ref[h], bias)
        for s in range(p):
            tot = jnp.sum(k_pages[s, hrows, :] * qh, axis=0, keepdims=True)
            zs[s] = jnp.where(sublane == h, tot, zs[s])
    z = jnp.concatenate([zs[s] + bias for s in range(p)], axis=0)
    soft = jnp.log1p(jnp.exp(-jnp.abs(z)))
    log_beta = jnp.minimum(z, 0.0) - soft
    hi, lo = _split2(log_beta - z)
    lb_scr[...] = log_beta
    hl_scr[...] = jnp.concatenate([hi, lo], axis=1)


def _paged_accumulate(v_pages, r1, first, lb_scr, carry_scr, acc_scr):
    p, nh, hd = PAGED_PAGES, SB_HEADS, SB_HEAD_DIM
    carry = jnp.where(first, 0.0, carry_scr[...])
    row_tot = r1[:, LANES:]
    prefix, run = [], carry
    for s in range(p):
        prefix.append(run)
        run = run + row_tot[s * nh:(s + 1) * nh]
    carry_scr[...] = run
    w = jnp.exp(lb_scr[...] + r1[:, :LANES] + jnp.concatenate(prefix, axis=0))
    for h in range(nh):
        hrows = slice(h * hd, (h + 1) * hd)
        acc = jnp.where(first, 0.0, acc_scr[hrows, :])
        for s in range(p):
            acc = acc + v_pages[s, hrows, :] * w[s * nh + h:s * nh + h + 1, :]
        acc_scr[hrows, :] = acc


def _post_paged_kernel(alpha, n_pages, pt_ref, bias_ref, *refs):
    post_refs = refs[:14]
    q_ref, lscan_ref, kc_hbm, vc_hbm = refs[14:18]
    y_ref, os_ref = refs[18:20]
    kbuf, vbuf, ksem, vsem, carry_scr, acc_scr, qb_scr, lb_scr, hl_scr = refs[20:]
    p, subs = PAGED_PAGES, POST_SEGMENTS
    groups = n_pages // p
    steps_per_seq = groups // subs
    i = pl.program_id(0)
    n_steps = pl.num_programs(0)
    seq = lax.div(i, steps_per_seq)
    g0 = lax.rem(i, steps_per_seq) * subs

    def copies(hbm, buf, sem, b, g, slot):
        return [pltpu.make_async_copy(hbm.at[pt_ref[b, n_pages - 1 - (g * p + s)]],
                                      buf.at[slot, s], sem.at[slot]) for s in range(p)]

    def start(cs):
        for c in cs:
            c.start()

    def wait(cs):
        for c in cs:
            c.wait()

    def finish_sequence(b):
        acc_t = acc_scr[...].T
        os_ref[pl.ds(b, 1), :] = jnp.sum(acc_t, axis=0, keepdims=True)

    scanned = {}

    def before(j):
        slot, other = j % 2, (j + 1) % 2
        if j == 0:
            @pl.when(i == 0)
            def _():
                hl_scr[...] = jnp.zeros_like(hl_scr)
                lb_scr[...] = jnp.zeros_like(lb_scr)
                carry_scr[...] = jnp.zeros_like(carry_scr)
                acc_scr[...] = jnp.zeros_like(acc_scr)
                vbuf[other] = jnp.zeros(vbuf.shape[1:], vbuf.dtype)
                start(copies(kc_hbm, kbuf, ksem, seq, g0, slot))

            @pl.when(i > 0)
            def _():
                wait(copies(vc_hbm, vbuf, vsem, seq, g0, other))
        else:
            wait(copies(vc_hbm, vbuf, vsem, seq, g0 + j - 1, other))
        wait(copies(kc_hbm, kbuf, ksem, seq, g0 + j, slot))
        if j < subs - 1:
            start(copies(kc_hbm, kbuf, ksem, seq, g0 + j + 1, other))
        else:
            @pl.when(i + 1 < n_steps)
            def _():
                nxt = i + 1
                start(copies(kc_hbm, kbuf, ksem, lax.div(nxt, steps_per_seq),
                             lax.rem(nxt, steps_per_seq) * subs, other))
        start(copies(vc_hbm, vbuf, vsem, seq, g0 + j, slot))
        scanned[j] = _dot(hl_scr[...], lscan_ref[...])

    def after(j):
        slot, other = j % 2, (j + 1) % 2
        if j == 0:
            _paged_accumulate(vbuf.at[other], scanned[j], False, lb_scr, carry_scr, acc_scr)

            @pl.when(jnp.logical_and(g0 == 0, i > 0))
            def _():
                finish_sequence(seq - 1)

            @pl.when(g0 == 0)
            def _():
                q_row = q_ref[pl.ds(seq, 1), :]
                qb_scr[...] = jnp.broadcast_to(q_row, (LANES, q_row.shape[1])).T
        else:
            _paged_accumulate(vbuf.at[other], scanned[j], g0 + j - 1 == 0,
                              lb_scr, carry_scr, acc_scr)
        _paged_logits(kbuf.at[slot], qb_scr, bias_ref, lb_scr, hl_scr)

    def segment(j, fn):
        before(j)
        out = fn()
        after(j)
        return out

    _post_math(alpha, *post_refs, y_ref, segment=segment)

    @pl.when(i == n_steps - 1)
    def _():
        last = (subs - 1) % 2
        wait(copies(vc_hbm, vbuf, vsem, seq, g0 + subs - 1, last))
        r1 = _dot(hl_scr[...], lscan_ref[...])
        _paged_accumulate(vbuf.at[last], r1, False, lb_scr, carry_scr, acc_scr)
        finish_sequence(seq)


def _post_with_paged_attention(x2d, o_sb, o_g, mod, seq_len, alpha, post_w,
                               page_table, bias, q_s, cache_kT, cache_vT):
    wg, wua, wub, wo, ln1g, ln1b, wf1, wf2, ln2g, ln2b = post_w
    n, d = x2d.shape
    nseq, n_pages = page_table.shape
    p, width = PAGED_PAGES, SB_WIDTH
    tm = min(TOKEN_TILE, seq_len)
    tiles_per_seq = seq_len // tm
    steps = n // tm
    assert cache_kT.shape[1:] == (width, LANES) and n_pages % p == 0
    assert (n_pages // p) % POST_SEGMENTS == 0 and steps * POST_SEGMENTS * p == nseq * n_pages
    lscan = _lane_scan_matrix()

    row = lambda cols: pl.BlockSpec((tm, cols), lambda i, pt: (i, 0))
    const = lambda shape: pl.BlockSpec(shape, lambda i, pt: (0,) * len(shape))
    single = lambda a: pl.BlockSpec(a.shape, lambda i, pt: (0,) * a.ndim,
                                    pipeline_mode=pl.Buffered(1))
    vec = lambda a: a.reshape(1, d)
    hbm = pl.BlockSpec(memory_space=pl.ANY)
    grid_spec = pltpu.PrefetchScalarGridSpec(
        num_scalar_prefetch=1,
        grid=(steps,),
        in_specs=[pl.BlockSpec(memory_space=pltpu.SMEM),
                  row(d), row(SB_WIDTH), row(GLA_V_WIDTH),
                  pl.BlockSpec((1, 6, 1, d), lambda i, pt: (i // tiles_per_seq, 0, 0, 0)),
                  single(wg), single(wua), single(wub), single(wo),
                  const((1, d)), const((1, d)), single(wf1), single(wf2),
                  const((1, d)), const((1, d)),
                  const(q_s.shape), const(lscan.shape), hbm, hbm],
        out_specs=(row(d), const((nseq, width))),
        scratch_shapes=[pltpu.VMEM((2, p, width, LANES), F32),
                        pltpu.VMEM((2, p, width, LANES), F32),
                        pltpu.SemaphoreType.DMA((2,)),
                        pltpu.SemaphoreType.DMA((2,)),
                        pltpu.VMEM((SB_HEADS, LANES), F32),
                        pltpu.VMEM((width, LANES), F32),
                        pltpu.VMEM((width, LANES), F32),
                        pltpu.VMEM((p * SB_HEADS, LANES), F32),
                        pltpu.VMEM((p * SB_HEADS, 2 * LANES), BF16)],
    )
    return pl.pallas_call(
        functools.partial(_post_paged_kernel, alpha, n_pages),
        grid_spec=grid_spec,
        out_shape=(jax.ShapeDtypeStruct((n, d), F32), jax.ShapeDtypeStruct((nseq, width), F32)),
        compiler_params=_cparams("arbitrary"),
        name="post_paged",
    )(page_table, bias, x2d, o_sb, o_g, mod, wg, wua, wub, wo, vec(ln1g), vec(ln1b), wf1, wf2,
      vec(ln2g), vec(ln2b), q_s, lscan, cache_kT, cache_vT)


def _gla_sample_kernel(gq_ref, gk_ref, la_ref, gv_ref, gr_ref, ng_ref, s0_ref, og_ref, st_ref):
    nh, dk, dv = GLA_HEADS, GLA_HEAD_K, GLA_HEAD_V
    la = la_ref[0]
    q_dec = gq_ref[0] * (dk ** -0.5) * jnp.exp(la)
    k_inv = gk_ref[0] * jnp.exp(-la)
    k_end = gk_ref[0]
    s0 = s0_ref[0]
    v8 = gv_ref[0]
    qk = q_dec * k_inv
    qs = q_dec * s0
    outs = []
    for h in range(nh):
        rows = slice(h * dk, (h + 1) * dk)
        att = jnp.sum(qk[rows], axis=0, keepdims=True)
        outs.append(att * v8[h:h + 1] + jnp.sum(qs[rows], axis=0, keepdims=True))
    o = jnp.concatenate(outs + [jnp.zeros((8 - nh, dv), F32)], axis=0)
    o = o * lax.rsqrt(jnp.mean(o * o, axis=-1, keepdims=True) + GN_EPS) * ng_ref[...]
    r = gr_ref[0]
    og_ref[0] = o * (r * _sigmoid(r))
    v_rows = jnp.concatenate(
        [jnp.broadcast_to(v8[h:h + 1], (dk, dv)) for h in range(nh)], axis=0)
    st_ref[0] = jnp.exp(la) * s0 + k_end * v_rows


def _gla_sample(gq, gk, gv, la, gr, norm_g, s0):
    n = gq.shape[0]
    nh, dk, dv = GLA_HEADS, GLA_HEAD_K, GLA_HEAD_V
    kw = nh * dk
    pad_heads = lambda a: jnp.pad(a.reshape(n, nh, dv), ((0, 0), (0, 8 - nh), (0, 0)))
    colspec = pl.BlockSpec((1, kw, 1), lambda b: (b, 0, 0))
    headspec = pl.BlockSpec((1, 8, dv), lambda b: (b, 0, 0))
    statespec = pl.BlockSpec((1, kw, dv), lambda b: (b, 0, 0))
    og, st = pl.pallas_call(
        _gla_sample_kernel,
        grid=(n,),
        in_specs=[colspec, colspec, colspec, headspec, headspec, _const_spec((1, dv)), statespec],
        out_specs=(headspec, statespec),
        out_shape=(jax.ShapeDtypeStruct((n, 8, dv), F32),
                   jax.ShapeDtypeStruct((n, kw, dv), F32)),
        compiler_params=_cparams("parallel"),
        name="gla_sample",
    )(gq.reshape(n, kw, 1), gk.reshape(n, kw, 1), la.reshape(n, kw, 1),
      pad_heads(gv), pad_heads(gr), norm_g.reshape(1, dv), s0)
    return og[:, :nh].reshape(n, nh * dv), st


def kernel(x_prompt, x_sample, cache_k, cache_v, state_gla, page_table, c_prompt, c_sample,
           w_ada, b_ada, w_in, sb_bias, w_gla_g2, b_gla_g, gla_norm_g, w_up_a, w_up_b, w_o,
           ln1_g, ln1_b, w_ff1, w_ff2, ln2_g, ln2_b):
    depth = w_ada.shape[0]
    pb, seq_len, d = x_prompt.shape
    sb_n = x_sample.shape[0]
    assert x_sample.shape[1] == 1
    alpha = (2.0 * depth) ** 0.25
    n_phys, page_size = cache_k.shape[1], cache_k.shape[2]

    yp = x_prompt.reshape(pb * seq_len, d)
    ys = x_sample.reshape(sb_n, d)
    c_all = jnp.concatenate([c_prompt, c_sample], axis=0)
    main_w = 3 * SB_WIDTH + 2 * GLA_K_WIDTH + 2 * GLA_V_WIDTH
    gate_off = main_w + GLA_GATE_RANK

    outs = [[] for _ in range(6)]
    for l in range(depth):
        ada = _ada(c_all, w_ada[l].astype(BF16), b_ada[l])
        mod_p = ada[:pb].reshape(pb, 6, 1, d)
        mod_s = ada[pb:].reshape(sb_n, 6, d).transpose(1, 0, 2).reshape(1, 6, sb_n, d)

        w_in_l = w_in[l]
        wm = w_in_l[:, :main_w].astype(BF16)
        wl = jnp.pad(w_in_l[:, main_w:gate_off], ((0, 0), (0, LANES - GLA_GATE_RANK))).astype(BF16)
        wg = w_in_l[:, gate_off:].astype(BF16)
        wg2 = jnp.pad(w_gla_g2[l], ((0, LANES - GLA_GATE_RANK), (0, 0))).astype(BF16)
        bg = b_gla_g[l].reshape(1, GLA_K_WIDTH)
        post_w = (wg, w_up_a[l].astype(BF16), w_up_b[l].astype(BF16), w_o[l].astype(BF16),
                  ln1_g[l], ln1_b[l], w_ff1[l].astype(BF16), w_ff2[l].astype(BF16),
                  ln2_g[l], ln2_b[l])

        qT3, k_p, v_p, k_bf, vT3, gq, gk, gv, gr, la = _proj_prompt(yp, mod_p, seq_len, wm, wl, wg2, bg)
        q_s, k_s, v_s, gq_s, gk_s, gv_s, gr_s, la_s = _proj_sample(ys, mod_s, wm, wl, wg2, bg)
        o_sb = _sb_prompt(sb_bias[l], qT3, k_bf, vT3, pb, seq_len)
        o_g, st_p = _gla_prompt(gq, gk, gv, la, gr, gla_norm_g[l], pb, seq_len)

        page_major = lambda c: jnp.transpose(c, (0, 2, 3, 1)).reshape(n_phys, SB_WIDTH, page_size)
        yp, o_sb_s = _post_with_paged_attention(
            yp, o_sb, o_g, mod_p, seq_len, alpha, post_w, page_table, sb_bias[l], q_s,
            page_major(cache_k[l]), page_major(cache_v[l]))

        s0 = state_gla[l].reshape(sb_n, GLA_HEADS * GLA_HEAD_K, GLA_HEAD_V)
        o_g_s, st_s = _gla_sample(gq_s, gk_s, gv_s, la_s, gr_s, gla_norm_g[l], s0)
        ys = _post(ys, o_sb_s, o_g_s, mod_s, 1, alpha, *post_w)

        rows_major = lambda t: jnp.transpose(
            t.reshape(pb, SB_HEADS, SB_HEAD_DIM, seq_len), (0, 3, 1, 2))
        outs[0].append(rows_major(k_p))
        outs[1].append(rows_major(v_p))
        outs[2].append(st_p.reshape(pb, GLA_HEADS, GLA_HEAD_K, GLA_HEAD_V))
        outs[3].append(k_s.reshape(sb_n, 1, SB_HEADS, SB_HEAD_DIM))
        outs[4].append(v_s.reshape(sb_n, 1, SB_HEADS, SB_HEAD_DIM))
        outs[5].append(st_s.reshape(sb_n, GLA_HEADS, GLA_HEAD_K, GLA_HEAD_V))

    stacked = [jnp.stack(o) for o in outs]
    return (yp.reshape(pb, seq_len, d), ys.reshape(sb_n, 1, d),
            stacked[0], stacked[1], stacked[2].astype(state_gla.dtype),
            stacked[3], stacked[4], stacked[5].astype(state_gla.dtype))
```

```python
import functools

import numpy as np
import jax
import jax.numpy as jnp
from jax import lax
from jax.experimental import pallas as pl
from jax.experimental.pallas import tpu as pltpu

F32 = jnp.float32
BF16 = jnp.bfloat16

SB_HEADS = 8
SB_HEAD_DIM = 64
SB_WIDTH = SB_HEADS * SB_HEAD_DIM
GLA_HEADS = 4
GLA_HEAD_K = 64
GLA_HEAD_V = 128
GLA_K_WIDTH = GLA_HEADS * GLA_HEAD_K
GLA_V_WIDTH = GLA_HEADS * GLA_HEAD_V
GLA_GATE_RANK = 16
GLA_TAU = 16.0
GLA_CHUNK = 64
LN_EPS = 1e-5
GN_EPS = 1e-6

LANES = 128
VMEM_LIMIT_BYTES = 56 * 1024 * 1024

TOKEN_TILE = 512
SB_Q_TILE = 256
SB_K_TILE = 128
SB_HEADS_PER_STEP = 8
GLA_BLOCK = 256
PAGED_PAGES = 8
FF_CHUNKS = (768, 768, 896, 896, 768)
PAGE_SLOTS = 3


def _cparams(*sem):
    return pltpu.CompilerParams(dimension_semantics=sem, vmem_limit_bytes=VMEM_LIMIT_BYTES)


def _dot(a, b):
    return jnp.dot(a, b, preferred_element_type=F32)


def _dot_nt(a, b):
    return lax.dot_general(a, b, (((1,), (1,)), ((), ())), preferred_element_type=F32)


def _ln_stats(x):
    mu = jnp.mean(x, axis=-1, keepdims=True)
    xc = x - mu
    var = jnp.mean(xc * xc, axis=-1, keepdims=True)
    return xc * lax.rsqrt(var + LN_EPS)


def _log_sigmoid(x):
    return jnp.minimum(x, 0.0) - jnp.log1p(jnp.exp(-jnp.abs(x)))


def _sigmoid(x):
    return 1.0 / (1.0 + jnp.exp(-x))


LOG2E = 1.4426950408889634


def _neg_abs(x):
    bits = lax.bitcast_convert_type(x, jnp.int32) | jnp.int32(-2 ** 31)
    return lax.bitcast_convert_type(bits, F32)


def _div_pow2(x, n):
    assert n & (n - 1) == 0
    return jnp.right_shift(x, n.bit_length() - 1)


def _split2(x):
    hi = x.astype(BF16)
    lo = (x - hi.astype(F32)).astype(BF16)
    return hi, lo


def _split3(x):
    hi = x.astype(BF16)
    r = x - hi.astype(F32)
    mid = r.astype(BF16)
    lo = (r - mid.astype(F32)).astype(BF16)
    return hi, mid, lo


def _ada_kernel(c_ref, w_ref, b_ref, o_ref):
    c = c_ref[...]
    a = c * _sigmoid(c)
    o_ref[...] = _dot(a.astype(BF16), w_ref[...]) + b_ref[...]


def _ada(c_all, w_ada_b, b_ada):
    n, d = c_all.shape
    nout = w_ada_b.shape[1]
    blk = d
    return pl.pallas_call(
        _ada_kernel,
        grid=(nout // blk,),
        in_specs=[pl.BlockSpec((n, d), lambda j: (0, 0)),
                  pl.BlockSpec((d, blk), lambda j: (0, j)),
                  pl.BlockSpec((1, blk), lambda j: (0, j))],
        out_specs=pl.BlockSpec((n, blk), lambda j: (0, j)),
        out_shape=jax.ShapeDtypeStruct((n, nout), F32),
        compiler_params=_cparams("parallel"),
        name="ada",
    )(c_all, w_ada_b, b_ada.reshape(1, nout))


def _proj_math(x, sh1, sc1, wm_ref, wl_ref, wg2_ref, bg_ref):
    h = (_ln_stats(x) * (1.0 + sc1) + sh1).astype(BF16)
    y = _dot(h, wm_ref[...])
    g_low = _dot(h, wl_ref[...])
    gl = _dot(g_low.astype(BF16), wg2_ref[...]) + bg_ref[...]
    log_a = _log_sigmoid(gl) * (1.0 / GLA_TAU)
    return y, log_a


def _proj_prompt_kernel(x_ref, mod_ref, wm_ref, wl_ref, wg2_ref, bg_ref,
                        qT_ref, kT_ref, vT_ref, kb_ref, vTb_ref, gq_ref, gk_ref, gv_ref, gr_ref, la_ref):
    y, log_a = _proj_math(x_ref[...], mod_ref[0, 0], mod_ref[0, 1], wm_ref, wl_ref, wg2_ref, bg_ref)
    w = SB_WIDTH
    q = y[:, 0:w] * (SB_HEAD_DIM ** -0.5 * LOG2E)
    k = y[:, w:2 * w]
    v = y[:, 2 * w:3 * w]
    kb_ref[...] = k.astype(BF16)
    qT = q.T.astype(BF16)
    vT = v.T
    kT_ref[0] = k.T
    vT_ref[0] = vT
    vT = vT.astype(BF16)
    for c in range(qT_ref.shape[0]):
        qT_ref[c] = qT[:, c * SB_Q_TILE:(c + 1) * SB_Q_TILE]
    for c in range(vTb_ref.shape[0]):
        vTb_ref[c] = vT[:, c * SB_K_TILE:(c + 1) * SB_K_TILE]
    o = 3 * w
    gq_ref[...] = y[:, o:o + GLA_K_WIDTH]
    gk_ref[...] = y[:, o + GLA_K_WIDTH:o + 2 * GLA_K_WIDTH]
    o += 2 * GLA_K_WIDTH
    gv_ref[...] = y[:, o:o + GLA_V_WIDTH].astype(BF16)
    gr_ref[...] = y[:, o + GLA_V_WIDTH:o + 2 * GLA_V_WIDTH]
    la_ref[...] = log_a


def _proj_sample_kernel(x_ref, mod_ref, wm_ref, wl_ref, wg2_ref, bg_ref,
                        q_ref, k_ref, v_ref, gq_ref, gk_ref, gv_ref, gr_ref, la_ref):
    y, log_a = _proj_math(x_ref[...], mod_ref[0, 0], mod_ref[0, 1], wm_ref, wl_ref, wg2_ref, bg_ref)
    w = SB_WIDTH
    q_ref[...] = y[:, 0:w] * (SB_HEAD_DIM ** -0.5)
    k_ref[...] = y[:, w:2 * w]
    v_ref[...] = y[:, 2 * w:3 * w]
    o = 3 * w
    gq_ref[...] = y[:, o:o + GLA_K_WIDTH]
    gk_ref[...] = y[:, o + GLA_K_WIDTH:o + 2 * GLA_K_WIDTH]
    o += 2 * GLA_K_WIDTH
    gv_ref[...] = y[:, o:o + GLA_V_WIDTH]
    gr_ref[...] = y[:, o + GLA_V_WIDTH:o + 2 * GLA_V_WIDTH]
    la_ref[...] = log_a


def _const_spec(shape):
    nd = len(shape)
    return pl.BlockSpec(shape, lambda *a: (0,) * nd)


def _proj_prompt(x2d, mod, seq_len, wm, wl, wg2, bg):
    n, d = x2d.shape
    tm = min(TOKEN_TILE, seq_len)
    assert seq_len % tm == 0 and tm % SB_Q_TILE == 0
    tiles_per_seq = seq_len // tm
    row = lambda cols: pl.BlockSpec((tm, cols), lambda i: (i, 0))
    dims_major = pl.BlockSpec((1, SB_WIDTH, tm),
                              lambda i: (i // tiles_per_seq, 0, i % tiles_per_seq))
    out_shape = (
        jax.ShapeDtypeStruct((n // SB_Q_TILE, SB_WIDTH, SB_Q_TILE), BF16),
        jax.ShapeDtypeStruct((n // seq_len, SB_WIDTH, seq_len), F32),
        jax.ShapeDtypeStruct((n // seq_len, SB_WIDTH, seq_len), F32),
        jax.ShapeDtypeStruct((n, SB_WIDTH), BF16),
        jax.ShapeDtypeStruct((n // SB_K_TILE, SB_WIDTH, SB_K_TILE), BF16),
        jax.ShapeDtypeStruct((n, GLA_K_WIDTH), F32),
        jax.ShapeDtypeStruct((n, GLA_K_WIDTH), F32),
        jax.ShapeDtypeStruct((n, GLA_V_WIDTH), BF16),
        jax.ShapeDtypeStruct((n, GLA_V_WIDTH), F32),
        jax.ShapeDtypeStruct((n, GLA_K_WIDTH), F32),
    )
    out_specs = (
        pl.BlockSpec((tm // SB_Q_TILE, SB_WIDTH, SB_Q_TILE), lambda i: (i, 0, 0)),
        dims_major, dims_major, row(SB_WIDTH),
        pl.BlockSpec((tm // SB_K_TILE, SB_WIDTH, SB_K_TILE), lambda i: (i, 0, 0)),
        row(GLA_K_WIDTH), row(GLA_K_WIDTH), row(GLA_V_WIDTH), row(GLA_V_WIDTH), row(GLA_K_WIDTH),
    )
    return pl.pallas_call(
        _proj_prompt_kernel,
        grid=(n // tm,),
        in_specs=[row(d),
                  pl.BlockSpec((1, 6, 1, d), lambda i: (i // tiles_per_seq, 0, 0, 0)),
                  _const_spec(wm.shape), _const_spec(wl.shape), _const_spec(wg2.shape),
                  _const_spec(bg.shape)],
        out_specs=out_specs,
        out_shape=out_shape,
        compiler_params=_cparams("parallel"),
        name="proj_prompt",
    )(x2d, mod, wm, wl, wg2, bg)


def _proj_sample(x2d, mod, wm, wl, wg2, bg):
    n, d = x2d.shape
    full = lambda cols: pl.BlockSpec((n, cols), lambda i: (0, 0))
    widths = (SB_WIDTH, SB_WIDTH, SB_WIDTH, GLA_K_WIDTH, GLA_K_WIDTH, GLA_V_WIDTH, GLA_V_WIDTH,
              GLA_K_WIDTH)
    return pl.pallas_call(
        _proj_sample_kernel,
        grid=(1,),
        in_specs=[full(d), _const_spec(mod.shape),
                  _const_spec(wm.shape), _const_spec(wl.shape), _const_spec(wg2.shape),
                  _const_spec(bg.shape)],
        out_specs=tuple(full(c) for c in widths),
        out_shape=tuple(jax.ShapeDtypeStruct((n, c), F32) for c in widths),
        compiler_params=_cparams("arbitrary"),
        name="proj_sample",
    )(x2d, mod, wm, wl, wg2, bg)


def _sb_scan_matrix():
    k = SB_K_TILE
    s = np.arange(k)[:, None]
    j = np.arange(k)[None, :]
    upper = (j > s).astype(np.float32)
    top = np.concatenate([upper, upper], axis=1)
    ones = np.ones((16, 2 * k), np.float32)
    return jnp.asarray(np.concatenate([top, ones], axis=0), dtype=BF16)


def _sb_prompt_kernel(bias_ref, qT_ref, k_ref, vT_ref, a_ref, o_ref):
    hg = pl.program_id(1)
    i = pl.program_id(2)
    tq, tk, hd, nh = SB_Q_TILE, SB_K_TILE, SB_HEAD_DIM, SB_HEADS_PER_STEP
    n_diag = tq // tk
    scan = a_ref[...]
    head_row = lax.broadcasted_iota(jnp.int32, (2 * hd, tq), 0)
    key_in_blk = lax.broadcasted_iota(jnp.int32, (tk, tq), 0)
    q_in_tile = lax.broadcasted_iota(jnp.int32, (tk, tq), 1)

    qms, biases = [], []
    for h in range(nh):
        pair = qT_ref[0, (h // 2) * 2 * hd:(h // 2 + 1) * 2 * hd, :].astype(F32)
        own = (head_row < hd) if h % 2 == 0 else (head_row >= hd)
        qms.append(jnp.where(own, pair, 0.0).astype(BF16))
        biases.append(bias_ref[nh * hg + h] * LOG2E)

    def key_group(first_kb, state, masked):
        carries, accs = list(state[:nh]), list(state[nh:])
        chains = [(first_kb + (n_diag - 1 - u), h) for u in range(n_diag) for h in range(nh)]
        zs = []
        for kb, h in chains:
            lanes = slice((h // 2) * 2 * hd, (h // 2 + 1) * 2 * hd)
            kblk = k_ref[pl.ds(pl.multiple_of(kb * tk, tk), tk), lanes]
            zs.append(_dot(kblk, qms[h]) + biases[h])
        log_betas, halves, valids = [], [], []
        for (kb, h), z in zip(chains, zs):
            soft = jnp.log(1.0 + jnp.exp2(_neg_abs(z))) * LOG2E
            log_beta = jnp.minimum(z, 0.0) - soft
            log_keep = log_beta - z
            valid = None
            if masked:
                valid = (kb * tk + key_in_blk) < (i * tq + q_in_tile)
                log_keep = jnp.where(valid, log_keep, 0.0)
            hi, lo = _split2(log_keep)
            log_betas.append(log_beta)
            valids.append(valid)
            halves.append(jnp.concatenate([hi, lo], axis=0))
        rs = [_dot(scan, hl) for hl in halves]
        ws = []
        for (kb, h), log_beta, r, valid in zip(chains, log_betas, rs, valids):
            arg = (log_beta + r[:tk]).reshape(tk // 8, 8, tq) + carries[h]
            w = jnp.exp2(arg).reshape(tk, tq)
            if masked:
                w = jnp.where(valid, w, 0.0)
            ws.append(w.astype(BF16))
            carries[h] = carries[h] + r[tk:tk + 8]
        for (kb, h), w in zip(chains, ws):
            accs[h] = accs[h] + _dot(vT_ref[kb, h * hd:(h + 1) * hd, :], w)
        return tuple(carries) + tuple(accs)

    state = tuple([jnp.zeros((8, tq), F32)] * nh + [jnp.zeros((hd, tq), F32)] * nh)
    state = key_group(n_diag * i, state, True)
    state = lax.fori_loop(0, i, lambda t, st: key_group(n_diag * (i - 1 - t), st, False), state)

    oT = jnp.concatenate(state[nh:], axis=0)
    o_ref[...] = oT.T.astype(o_ref.dtype)


def _sb_prompt(bias, qT3, k_bf, vT3, batch, seq_len):
    n = k_bf.shape[0]
    tq, tk = SB_Q_TILE, SB_K_TILE
    assert seq_len % tq == 0
    qt = seq_len // tq
    kblocks = seq_len // tk
    width = SB_HEADS_PER_STEP * SB_HEAD_DIM
    scan = _sb_scan_matrix()
    return pl.pallas_call(
        _sb_prompt_kernel,
        grid=(batch, SB_HEADS // SB_HEADS_PER_STEP, qt),
        in_specs=[pl.BlockSpec(memory_space=pltpu.SMEM),
                  pl.BlockSpec((1, width, tq), lambda b, hg, i: (b * qt + i, hg, 0)),
                  pl.BlockSpec((seq_len, width), lambda b, hg, i: (b, hg)),
                  pl.BlockSpec((kblocks, width, tk), lambda b, hg, i: (b, hg, 0)),
                  _const_spec(scan.shape)],
        out_specs=pl.BlockSpec((tq, width), lambda b, hg, i: (b * qt + i, hg)),
        out_shape=jax.ShapeDtypeStruct((n, SB_WIDTH), BF16),
        compiler_params=_cparams("parallel", "parallel", "arbitrary"),
        name="sb_prompt",
    )(bias, qT3, k_bf, vT3, scan)


def _gla_consts():
    t = np.arange(GLA_BLOCK)
    same = (t[:, None] // GLA_CHUNK) == (t[None, :] // GLA_CHUNK)
    tri = (same & (t[None, :] <= t[:, None])).astype(np.float32)
    ones = same.astype(np.float32)
    scan = np.concatenate([np.concatenate([tri] * 3, axis=1),
                           np.concatenate([ones] * 3, axis=1)], axis=0)
    n_chunks = GLA_BLOCK // GLA_CHUNK
    ind = np.zeros((GLA_BLOCK, n_chunks * LANES), np.float32)
    for c in range(n_chunks):
        ind[c * GLA_CHUNK:(c + 1) * GLA_CHUNK, c * LANES:(c + 1) * LANES] = 1.0
    ind = np.concatenate([ind] * 3, axis=0)
    return jnp.asarray(scan, dtype=BF16), jnp.asarray(ind, dtype=BF16)


def _gla_prompt_kernel(gq_ref, gk_ref, gv_ref, la_ref, gr_ref, ng_ref, scan_ref, ind_ref,
                       og_ref, st_ref, s_scr):
    j = pl.program_id(1)
    tb, c_len = GLA_BLOCK, GLA_CHUNK
    n_chunks = tb // c_len
    dk, dv, nh = GLA_HEAD_K, GLA_HEAD_V, GLA_HEADS

    @pl.when(j == 0)
    def _():
        s_scr[...] = jnp.zeros_like(s_scr)

    la = la_ref[...]
    parts = jnp.concatenate(_split3(la), axis=0)
    bb = _dot(scan_ref[...], parts)
    b = bb[:tb]
    b_last = bb[tb:]
    laT_parts = jnp.concatenate(_split3(la.T), axis=1)
    b_last_col = _dot(laT_parts, ind_ref[...])

    q = gq_ref[...] * (dk ** -0.5)
    k = gk_ref[...]
    v = gv_ref[...]
    q_dec = q * jnp.exp(b)
    k_inv = (k * jnp.exp(-b)).astype(BF16)
    k_endT = (k * jnp.exp(b_last - b)).T

    lane_head = _div_pow2(lax.broadcasted_iota(jnp.int32, (tb, nh * dk), 1), dk)
    ti = lax.broadcasted_iota(jnp.int32, (tb, tb), 0)
    tj = lax.broadcasted_iota(jnp.int32, (tb, tb), 1)
    tj_chunk = _div_pow2(tj, c_len)
    causal = (_div_pow2(ti, c_len) == tj_chunk) & (tj <= ti)

    qm = [jnp.where(lane_head == h, q_dec, 0.0).astype(BF16) for h in range(nh)]
    scores = [_dot_nt(qm[h], k_inv) for h in range(nh)]
    us = []
    for c in range(n_chunks):
        k_endT_c = jnp.where(tj_chunk == c, k_endT, 0.0).astype(BF16)
        us.append(_dot(k_endT_c, v))
    o_intra = [_dot(jnp.where(causal, scores[h], 0.0).astype(BF16), v[:, h * dv:(h + 1) * dv])
               for h in range(nh)]

    states = [s_scr[...]]
    for c in range(n_chunks):
        u_diag = jnp.concatenate(
            [us[c][h * dk:(h + 1) * dk, h * dv:(h + 1) * dv] for h in range(nh)], axis=0)
        states.append(jnp.exp(b_last_col[:, c * LANES:(c + 1) * LANES]) * states[c] + u_diag)
    o_inter = []
    for c in range(n_chunks):
        rows = slice(c * c_len, (c + 1) * c_len)
        q_stack = jnp.concatenate([qm[h][rows] for h in range(nh)], axis=0)
        o_inter.append(_dot(q_stack, states[c].astype(BF16)))

    norm_g = ng_ref[...]
    for c in range(n_chunks):
        rows = slice(c * c_len, (c + 1) * c_len)
        for h in range(nh):
            o = o_intra[h][rows] + o_inter[c][h * c_len:(h + 1) * c_len]
            o = o * lax.rsqrt(jnp.mean(o * o, axis=-1, keepdims=True) + GN_EPS) * norm_g
            r = gr_ref[rows, h * dv:(h + 1) * dv]
            og_ref[rows, h * dv:(h + 1) * dv] = (o * (r * _sigmoid(r))).astype(og_ref.dtype)
    s_all = states[n_chunks]
    s_scr[...] = s_all

    @pl.when(j == pl.num_programs(1) - 1)
    def _():
        st_ref[0] = s_all


def _gla_prompt(gq, gk, gv, la, gr, norm_g, batch, seq_len):
    n = gq.shape[0]
    tb = GLA_BLOCK
    assert seq_len % tb == 0
    nb = seq_len // tb
    scan, ind = _gla_consts()
    row = lambda cols: pl.BlockSpec((tb, cols), lambda b, j: (b * nb + j, 0))
    rows_s = GLA_HEADS * GLA_HEAD_K
    return pl.pallas_call(
        _gla_prompt_kernel,
        grid=(batch, nb),
        in_specs=[row(GLA_K_WIDTH), row(GLA_K_WIDTH), row(GLA_V_WIDTH), row(GLA_K_WIDTH),
                  row(GLA_V_WIDTH), _const_spec((1, GLA_HEAD_V)),
                  _const_spec(scan.shape), _const_spec(ind.shape)],
        out_specs=(row(GLA_V_WIDTH),
                   pl.BlockSpec((1, rows_s, GLA_HEAD_V), lambda b, j: (b, 0, 0))),
        out_shape=(jax.ShapeDtypeStruct((n, GLA_V_WIDTH), BF16),
                   jax.ShapeDtypeStruct((batch, rows_s, GLA_HEAD_V), F32)),
        scratch_shapes=[pltpu.VMEM((rows_s, GLA_HEAD_V), F32)],
        compiler_params=_cparams("parallel", "arbitrary"),
        name="gla_prompt",
    )(gq, gk, gv, la, gr, norm_g.reshape(1, GLA_HEAD_V), scan, ind)


POST_SEGMENTS = 8


def _post_math(alpha, x_ref, osb_ref, og_ref, mod_ref, wg_ref, wua_ref, wub_ref, wo_ref,
               ln1g_ref, ln1b_ref, wf1_ref, wf2_ref, ln2g_ref, ln2b_ref, y_ref, segment):
    x = x_ref[...]
    sh1, sc1, gt1, sh2, sc2, gt2 = [mod_ref[0, m] for m in range(6)]
    d = x.shape[-1]
    assert sum(FF_CHUNKS) == wf1_ref.shape[1] and POST_SEGMENTS == 3 + len(FF_CHUNKS)

    def seg0():
        h = (_ln_stats(x) * (1.0 + sc1) + sh1).astype(BF16)
        return h, _sigmoid(_dot(h, wg_ref[:, :d]))

    h, gate_a = segment(0, seg0)
    part_a, gate_b = segment(1, lambda: (gate_a * _dot(osb_ref[...].astype(BF16), wua_ref[...]),
                                         _sigmoid(_dot(h, wg_ref[:, d:]))))

    def seg2():
        merged = (part_a + gate_b * _dot(og_ref[...].astype(BF16), wub_ref[...])).astype(BF16)
        x1 = _ln_stats(alpha * x + gt1 * _dot(merged, wo_ref[...])) * ln1g_ref[...] + ln1b_ref[...]
        return x1, (_ln_stats(x1) * (1.0 + sc2) + sh2).astype(BF16)

    x1, h2 = segment(2, seg2)
    f = jnp.zeros_like(x)
    lo = 0
    for c, width in enumerate(FF_CHUNKS):
        cols = slice(lo, lo + width)
        lo += width

        def ff(f=f, cols=cols, last=(c == len(FF_CHUNKS) - 1)):
            u = jnp.maximum(_dot(h2, wf1_ref[:, cols]), 0.0)
            f_new = f + _dot((u * u).astype(BF16), wf2_ref[cols, :])
            if last:
                y_ref[...] = _ln_stats(alpha * x1 + gt2 * f_new) * ln2g_ref[...] + ln2b_ref[...]
            return f_new

        f = segment(3 + c, ff)


def _post_kernel(alpha, *refs):
    _post_math(alpha, *refs, segment=lambda j, fn: fn())


def _post(x2d, o_sb, o_g, mod, seq_len, alpha, wg, wua, wub, wo, ln1g, ln1b, wf1, wf2, ln2g, ln2b):
    n, d = x2d.shape
    per_token = mod.shape[2] != 1
    tm = n if per_token else min(TOKEN_TILE, seq_len)
    tiles_per_seq = 1 if per_token else seq_len // tm
    row = lambda cols: pl.BlockSpec((tm, cols), lambda i: (i, 0))
    single = lambda a: pl.BlockSpec(a.shape, lambda i: (0,) * a.ndim, pipeline_mode=pl.Buffered(1))
    vec = lambda a: a.reshape(1, d)
    mod_spec = (_const_spec(mod.shape) if per_token else
                pl.BlockSpec((1, 6, 1, d), lambda i: (i // tiles_per_seq, 0, 0, 0)))
    return pl.pallas_call(
        functools.partial(_post_kernel, alpha),
        grid=(n // tm,),
        in_specs=[row(d), row(SB_WIDTH), row(GLA_V_WIDTH), mod_spec,
                  single(wg), single(wua), single(wub), single(wo),
                  _const_spec((1, d)), _const_spec((1, d)),
                  single(wf1), single(wf2),
                  _const_spec((1, d)), _const_spec((1, d))],
        out_specs=row(d),
        out_shape=jax.ShapeDtypeStruct((n, d), F32),
        compiler_params=_cparams("parallel"),
        name="post",
    )(x2d, o_sb, o_g, mod, wg, wua, wub, wo, vec(ln1g), vec(ln1b), wf1, wf2, vec(ln2g), vec(ln2b))


def _lane_scan_matrix():
    j = np.arange(LANES)[:, None]
    s = np.arange(LANES)[None, :]
    later = (j > s).astype(np.float32)
    half = np.concatenate([later, np.ones((LANES, LANES), np.float32)], axis=1)
    return jnp.asarray(np.concatenate([half, half], axis=0), dtype=BF16)


def _paged_logits(k_pages, qb_scr, bias_ref, lb_scr, hl_scr):
    p, nh, hd = PAGED_PAGES, SB_HEADS, SB_HEAD_DIM
    sublane = lax.broadcasted_iota(jnp.int32, (nh, LANES), 0)
    zs = [jnp.zeros((nh, LANES), F32)] * p
    bias = jnp.zeros((nh, LANES), F32)
    for h in range(nh):
        hrows = slice(h * hd, (h + 1) * hd)
        qh = qb_scr[hrows, :]
        bias = jnp.where(sublane == h, bias_ref[h], bias)
        for s in range(p):
            tot = jnp.sum(k_pages[s, hrows, :] * qh, axis=0, keepdims=True)
            zs[s] = jnp.where(sublane == h, tot, zs[s])
    z = jnp.concatenate([zs[s] + bias for s in range(p)], axis=0)
    soft = jnp.log1p(jnp.exp(-jnp.abs(z)))
    log_beta = jnp.minimum(z, 0.0) - soft
    hi, lo = _split2(log_beta - z)
    lb_scr[...] = log_beta
    hl_scr[...] = jnp.concatenate([hi, lo], axis=1)


def _paged_accumulate(v_pages, r1, first, lb_scr, carry_scr, acc_scr):
    p, nh, hd = PAGED_PAGES, SB_HEADS, SB_HEAD_DIM
    carry = jnp.where(first, 0.0, carry_scr[...])
    row_tot = r1[:, LANES:]
    prefix, run = [], carry
    for s in range(p):
        prefix.append(run)
        run = run + row_tot[s * nh:(s + 1) * nh]
    carry_scr[...] = run
    w = jnp.exp(lb_scr[...] + r1[:, :LANES] + jnp.concatenate(prefix, axis=0))
    for h in range(nh):
        hrows = slice(h * hd, (h + 1) * hd)
        acc = jnp.where(first, 0.0, acc_scr[hrows, :])
        for s in range(p):
            acc = acc + v_pages[s, hrows, :] * w[s * nh + h:s * nh + h + 1, :]
        acc_scr[hrows, :] = acc


def _post_paged_kernel(alpha, n_pages, pt_ref, bias_ref, *refs):
    post_refs = refs[:14]
    q_ref, lscan_ref, kc_hbm, vc_hbm = refs[14:18]
    y_ref, os_ref = refs[18:20]
    kbuf, vbuf, ksem, vsem, carry_scr, acc_scr, qb_scr, lb_scr, hl_scr = refs[20:]
    p, subs, slots = PAGED_PAGES, POST_SEGMENTS, PAGE_SLOTS
    groups = n_pages // p
    i = pl.program_id(0)
    n_steps = pl.num_programs(0)

    def slot_of(t):
        return lax.rem(t + slots, slots)

    def copies(hbm, buf, sem, t):
        b, g, slot = lax.div(t, groups), lax.rem(t, groups), slot_of(t)
        return [pltpu.make_async_copy(hbm.at[pt_ref[b, n_pages - 1 - (g * p + s)]],
                                      buf.at[slot, s], sem.at[slot]) for s in range(p)]

    def start(cs):
        for c in cs:
            c.start()

    def wait(cs):
        for c in cs:
            c.wait()

    def finish_sequence(b):
        acc_t = acc_scr[...].T
        os_ref[pl.ds(b, 1), :] = jnp.sum(acc_t, axis=0, keepdims=True)

    scanned = {}

    def before(j):
        t = subs * i + j
        if j == 0:
            @pl.when(i == 0)
            def _():
                hl_scr[...] = jnp.zeros_like(hl_scr)
                lb_scr[...] = jnp.zeros_like(lb_scr)
                carry_scr[...] = jnp.zeros_like(carry_scr)
                acc_scr[...] = jnp.zeros_like(acc_scr)
                vbuf[slots - 1] = jnp.zeros(vbuf.shape[1:], vbuf.dtype)
                start(copies(kc_hbm, kbuf, ksem, t))
                start(copies(kc_hbm, kbuf, ksem, t + 1))
                start(copies(vc_hbm, vbuf, vsem, t))

            @pl.when(i > 0)
            def _():
                wait(copies(vc_hbm, vbuf, vsem, t - 1))
        else:
            wait(copies(vc_hbm, vbuf, vsem, t - 1))
        wait(copies(kc_hbm, kbuf, ksem, t))
        for ahead, hbm, buf, sem in ((2, kc_hbm, kbuf, ksem), (1, vc_hbm, vbuf, vsem)):
            if j + ahead < subs:
                start(copies(hbm, buf, sem, t + ahead))
            else:
                @pl.when(i + 1 < n_steps)
                def _(ahead=ahead, hbm=hbm, buf=buf, sem=sem):
                    start(copies(hbm, buf, sem, t + ahead))
        scanned[j] = _dot(hl_scr[...], lscan_ref[...])

    def after(j):
        t = subs * i + j
        first = lax.rem(t - 1 + groups, groups) == 0
        _paged_accumulate(vbuf.at[slot_of(t - 1)], scanned[j], first, lb_scr, carry_scr, acc_scr)
        if j == 0:
            @pl.when(jnp.logical_and(lax.rem(t, groups) == 0, t > 0))
            def _():
                finish_sequence(lax.div(t, groups) - 1)

            @pl.when(lax.rem(t, groups) == 0)
            def _():
                q_row = q_ref[pl.ds(lax.div(t, groups), 1), :]
                qb_scr[...] = jnp.broadcast_to(q_row, (LANES, q_row.shape[1])).T
        _paged_logits(kbuf.at[slot_of(t)], qb_scr, bias_ref, lb_scr, hl_scr)

    def segment(j, fn):
        before(j)
        out = fn()
        after(j)
        return out

    _post_math(alpha, *post_refs, y_ref, segment=segment)

    @pl.when(i == n_steps - 1)
    def _():
        t_last = subs * n_steps - 1
        wait(copies(vc_hbm, vbuf, vsem, t_last))
        r1 = _dot(hl_scr[...], lscan_ref[...])
        _paged_accumulate(vbuf.at[slot_of(t_last)], r1, False, lb_scr, carry_scr, acc_scr)
        finish_sequence(lax.div(t_last, groups))


def _post_with_paged_attention(x2d, o_sb, o_g, mod, seq_len, alpha, post_w,
                               page_table, bias, q_s, cache_kT, cache_vT):
    wg, wua, wub, wo, ln1g, ln1b, wf1, wf2, ln2g, ln2b = post_w
    n, d = x2d.shape
    nseq, n_pages = page_table.shape
    p, width = PAGED_PAGES, SB_WIDTH
    tm = min(TOKEN_TILE, seq_len)
    tiles_per_seq = seq_len // tm
    steps = n // tm
    assert cache_kT.shape[1:] == (width, LANES) and n_pages % p == 0
    assert (n_pages // p) % POST_SEGMENTS == 0 and steps * POST_SEGMENTS * p == nseq * n_pages
    lscan = _lane_scan_matrix()

    row = lambda cols: pl.BlockSpec((tm, cols), lambda i, pt: (i, 0))
    const = lambda shape: pl.BlockSpec(shape, lambda i, pt: (0,) * len(shape))
    single = lambda a: pl.BlockSpec(a.shape, lambda i, pt: (0,) * a.ndim,
                                    pipeline_mode=pl.Buffered(1))
    vec = lambda a: a.reshape(1, d)
    hbm = pl.BlockSpec(memory_space=pl.ANY)
    grid_spec = pltpu.PrefetchScalarGridSpec(
        num_scalar_prefetch=1,
        grid=(steps,),
        in_specs=[pl.BlockSpec(memory_space=pltpu.SMEM),
                  row(d), row(SB_WIDTH), row(GLA_V_WIDTH),
                  pl.BlockSpec((1, 6, 1, d), lambda i, pt: (i // tiles_per_seq, 0, 0, 0)),
                  single(wg), single(wua), single(wub), single(wo),
                  const((1, d)), const((1, d)), single(wf1), single(wf2),
                  const((1, d)), const((1, d)),
                  const(q_s.shape), const(lscan.shape), hbm, hbm],
        out_specs=(row(d), const((nseq, width))),
        scratch_shapes=[pltpu.VMEM((PAGE_SLOTS, p, width, LANES), F32),
                        pltpu.VMEM((PAGE_SLOTS, p, width, LANES), F32),
                        pltpu.SemaphoreType.DMA((PAGE_SLOTS,)),
                        pltpu.SemaphoreType.DMA((PAGE_SLOTS,)),
                        pltpu.VMEM((SB_HEADS, LANES), F32),
                        pltpu.VMEM((width, LANES), F32),
                        pltpu.VMEM((width, LANES), F32),
                        pltpu.VMEM((p * SB_HEADS, LANES), F32),
                        pltpu.VMEM((p * SB_HEADS, 2 * LANES), BF16)],
    )
    return pl.pallas_call(
        functools.partial(_post_paged_kernel, alpha, n_pages),
        grid_spec=grid_spec,
        out_shape=(jax.ShapeDtypeStruct((n, d), F32), jax.ShapeDtypeStruct((nseq, width), F32)),
        compiler_params=_cparams("arbitrary"),
        name="post_paged",
    )(page_table, bias, x2d, o_sb, o_g, mod, wg, wua, wub, wo, vec(ln1g), vec(ln1b), wf1, wf2,
      vec(ln2g), vec(ln2b), q_s, lscan, cache_kT, cache_vT)


def _gla_sample_kernel(gq_ref, gk_ref, la_ref, gv_ref, gr_ref, ng_ref, s0_ref, og_ref, st_ref):
    nh, dk, dv = GLA_HEADS, GLA_HEAD_K, GLA_HEAD_V
    la = la_ref[0]
    q_dec = gq_ref[0] * (dk ** -0.5) * jnp.exp(la)
    k_inv = gk_ref[0] * jnp.exp(-la)
    k_end = gk_ref[0]
    s0 = s0_ref[0]
    v8 = gv_ref[0]
    qk = q_dec * k_inv
    qs = q_dec * s0
    outs = []
    for h in range(nh):
        rows = slice(h * dk, (h + 1) * dk)
        att = jnp.sum(qk[rows], axis=0, keepdims=True)
        outs.append(att * v8[h:h + 1] + jnp.sum(qs[rows], axis=0, keepdims=True))
    o = jnp.concatenate(outs + [jnp.zeros((8 - nh, dv), F32)], axis=0)
    o = o * lax.rsqrt(jnp.mean(o * o, axis=-1, keepdims=True) + GN_EPS) * ng_ref[...]
    r = gr_ref[0]
    og_ref[0] = o * (r * _sigmoid(r))
    v_rows = jnp.concatenate(
        [jnp.broadcast_to(v8[h:h + 1], (dk, dv)) for h in range(nh)], axis=0)
    st_ref[0] = jnp.exp(la) * s0 + k_end * v_rows


def _gla_sample(gq, gk, gv, la, gr, norm_g, s0):
    n = gq.shape[0]
    nh, dk, dv = GLA_HEADS, GLA_HEAD_K, GLA_HEAD_V
    kw = nh * dk
    pad_heads = lambda a: jnp.pad(a.reshape(n, nh, dv), ((0, 0), (0, 8 - nh), (0, 0)))
    colspec = pl.BlockSpec((1, kw, 1), lambda b: (b, 0, 0))
    headspec = pl.BlockSpec((1, 8, dv), lambda b: (b, 0, 0))
    statespec = pl.BlockSpec((1, kw, dv), lambda b: (b, 0, 0))
    og, st = pl.pallas_call(
        _gla_sample_kernel,
        grid=(n,),
        in_specs=[colspec, colspec, colspec, headspec, headspec, _const_spec((1, dv)), statespec],
        out_specs=(headspec, statespec),
        out_shape=(jax.ShapeDtypeStruct((n, 8, dv), F32),
                   jax.ShapeDtypeStruct((n, kw, dv), F32)),
        compiler_params=_cparams("parallel"),
        name="gla_sample",
    )(gq.reshape(n, kw, 1), gk.reshape(n, kw, 1), la.reshape(n, kw, 1),
      pad_heads(gv), pad_heads(gr), norm_g.reshape(1, dv), s0)
    return og[:, :nh].reshape(n, nh * dv), st


def kernel(x_prompt, x_sample, cache_k, cache_v, state_gla, page_table, c_prompt, c_sample,
           w_ada, b_ada, w_in, sb_bias, w_gla_g2, b_gla_g, gla_norm_g, w_up_a, w_up_b, w_o,
           ln1_g, ln1_b, w_ff1, w_ff2, ln2_g, ln2_b):
    depth = w_ada.shape[0]
    pb, seq_len, d = x_prompt.shape
    sb_n = x_sample.shape[0]
    assert x_sample.shape[1] == 1
    alpha = (2.0 * depth) ** 0.25
    n_phys, page_size = cache_k.shape[1], cache_k.shape[2]

    yp = x_prompt.reshape(pb * seq_len, d)
    ys = x_sample.reshape(sb_n, d)
    c_all = jnp.concatenate([c_prompt, c_sample], axis=0)
    main_w = 3 * SB_WIDTH + 2 * GLA_K_WIDTH + 2 * GLA_V_WIDTH
    gate_off = main_w + GLA_GATE_RANK

    outs = [[] for _ in range(6)]
    for l in range(depth):
        ada = _ada(c_all, w_ada[l].astype(BF16), b_ada[l])
        mod_p = ada[:pb].reshape(pb, 6, 1, d)
        mod_s = ada[pb:].reshape(sb_n, 6, d).transpose(1, 0, 2).reshape(1, 6, sb_n, d)

        w_in_l = w_in[l]
        wm = w_in_l[:, :main_w].astype(BF16)
        wl = jnp.pad(w_in_l[:, main_w:gate_off], ((0, 0), (0, LANES - GLA_GATE_RANK))).astype(BF16)
        wg = w_in_l[:, gate_off:].astype(BF16)
        wg2 = jnp.pad(w_gla_g2[l], ((0, LANES - GLA_GATE_RANK), (0, 0))).astype(BF16)
        bg = b_gla_g[l].reshape(1, GLA_K_WIDTH)
        post_w = (wg, w_up_a[l].astype(BF16), w_up_b[l].astype(BF16), w_o[l].astype(BF16),
                  ln1_g[l], ln1_b[l], w_ff1[l].astype(BF16), w_ff2[l].astype(BF16),
                  ln2_g[l], ln2_b[l])

        qT3, k_p, v_p, k_bf, vT3, gq, gk, gv, gr, la = _proj_prompt(yp, mod_p, seq_len, wm, wl, wg2, bg)
        q_s, k_s, v_s, gq_s, gk_s, gv_s, gr_s, la_s = _proj_sample(ys, mod_s, wm, wl, wg2, bg)
        o_sb = _sb_prompt(sb_bias[l], qT3, k_bf, vT3, pb, seq_len)
        o_g, st_p = _gla_prompt(gq, gk, gv, la, gr, gla_norm_g[l], pb, seq_len)

        page_major = lambda c: jnp.transpose(c, (0, 2, 3, 1)).reshape(n_phys, SB_WIDTH, page_size)
        yp, o_sb_s = _post_with_paged_attention(
            yp, o_sb, o_g, mod_p, seq_len, alpha, post_w, page_table, sb_bias[l], q_s,
            page_major(cache_k[l]), page_major(cache_v[l]))

        s0 = state_gla[l].reshape(sb_n, GLA_HEADS * GLA_HEAD_K, GLA_HEAD_V)
        o_g_s, st_s = _gla_sample(gq_s, gk_s, gv_s, la_s, gr_s, gla_norm_g[l], s0)
        ys = _post(ys, o_sb_s, o_g_s, mod_s, 1, alpha, *post_w)

        rows_major = lambda t: jnp.transpose(
            t.reshape(pb, SB_HEADS, SB_HEAD_DIM, seq_len), (0, 3, 1, 2))
        outs[0].append(rows_major(k_p))
        outs[1].append(rows_major(v_p))
        outs[2].append(st_p.reshape(pb, GLA_HEADS, GLA_HEAD_K, GLA_HEAD_V))
        outs[3].append(k_s.reshape(sb_n, 1, SB_HEADS, SB_HEAD_DIM))
        outs[4].append(v_s.reshape(sb_n, 1, SB_HEADS, SB_HEAD_DIM))
        outs[5].append(st_s.reshape(sb_n, GLA_HEADS, GLA_HEAD_K, GLA_HEAD_V))

    stacked = [jnp.stack(o) for o in outs]
    return (yp.reshape(pb, seq_len, d), ys.reshape(sb_n, 1, d),
            stacked[0], stacked[1], stacked[2].astype(state_gla.dtype),
            stacked[3], stacked[4], stacked[5].astype(state_gla.dtype))
```

```python
import functools

import numpy as np
import jax
import jax.numpy as jnp
from jax import lax
from jax.experimental import pallas as pl
from jax.experimental.pallas import tpu as pltpu

F32 = jnp.float32
BF16 = jnp.bfloat16

SB_HEADS = 8
SB_HEAD_DIM = 64
SB_WIDTH = SB_HEADS * SB_HEAD_DIM
GLA_HEADS = 4
GLA_HEAD_K = 64
GLA_HEAD_V = 128
GLA_K_WIDTH = GLA_HEADS * GLA_HEAD_K
GLA_V_WIDTH = GLA_HEADS * GLA_HEAD_V
GLA_GATE_RANK = 16
GLA_TAU = 16.0
GLA_CHUNK = 64
LN_EPS = 1e-5
GN_EPS = 1e-6

LANES = 128
VMEM_LIMIT_BYTES = 56 * 1024 * 1024

TOKEN_TILE = 512
SB_Q_TILE = 256
SB_K_TILE = 128
SB_CHAIN_SKEW = 8
SB_HEADS_PER_STEP = 8
GLA_BLOCK = 256
PAGED_PAGES = 8
FF_CHUNKS = (768, 768, 768, 768, 1024)
PAGE_SLOTS = 3


def _cparams(*sem):
    return pltpu.CompilerParams(dimension_semantics=sem, vmem_limit_bytes=VMEM_LIMIT_BYTES)


def _dot(a, b):
    return jnp.dot(a, b, preferred_element_type=F32)


def _dot_nt(a, b):
    return lax.dot_general(a, b, (((1,), (1,)), ((), ())), preferred_element_type=F32)


def _ln_stats(x):
    mu = jnp.mean(x, axis=-1, keepdims=True)
    xc = x - mu
    var = jnp.mean(xc * xc, axis=-1, keepdims=True)
    return xc * lax.rsqrt(var + LN_EPS)


def _log_sigmoid(x):
    return jnp.minimum(x, 0.0) - jnp.log1p(jnp.exp(-jnp.abs(x)))


def _sigmoid(x):
    return 1.0 / (1.0 + jnp.exp(-x))


LOG2E = 1.4426950408889634


def _neg_abs(x):
    bits = lax.bitcast_convert_type(x, jnp.int32) | jnp.int32(-2 ** 31)
    return lax.bitcast_convert_type(bits, F32)


def _div_pow2(x, n):
    assert n & (n - 1) == 0
    return jnp.right_shift(x, n.bit_length() - 1)


def _split2(x):
    hi = x.astype(BF16)
    lo = (x - hi.astype(F32)).astype(BF16)
    return hi, lo


def _split3(x):
    hi = x.astype(BF16)
    r = x - hi.astype(F32)
    mid = r.astype(BF16)
    lo = (r - mid.astype(F32)).astype(BF16)
    return hi, mid, lo


def _ada_kernel(c_ref, w_ref, b_ref, o_ref):
    c = c_ref[...]
    a = c * _sigmoid(c)
    o_ref[...] = _dot(a.astype(BF16), w_ref[...]) + b_ref[...]


def _ada(c_all, w_ada_b, b_ada):
    n, d = c_all.shape
    nout = w_ada_b.shape[1]
    blk = d
    return pl.pallas_call(
        _ada_kernel,
        grid=(nout // blk,),
        in_specs=[pl.BlockSpec((n, d), lambda j: (0, 0)),
                  pl.BlockSpec((d, blk), lambda j: (0, j)),
                  pl.BlockSpec((1, blk), lambda j: (0, j))],
        out_specs=pl.BlockSpec((n, blk), lambda j: (0, j)),
        out_shape=jax.ShapeDtypeStruct((n, nout), F32),
        compiler_params=_cparams("parallel"),
        name="ada",
    )(c_all, w_ada_b, b_ada.reshape(1, nout))


def _proj_math(x, sh1, sc1, wm_ref, wl_ref, wg2_ref, bg_ref):
    h = (_ln_stats(x) * (1.0 + sc1) + sh1).astype(BF16)
    y = _dot(h, wm_ref[...])
    g_low = _dot(h, wl_ref[...])
    gl = _dot(g_low.astype(BF16), wg2_ref[...]) + bg_ref[...]
    log_a = _log_sigmoid(gl) * (1.0 / GLA_TAU)
    return y, log_a


def _proj_prompt_kernel(x_ref, mod_ref, wm_ref, wl_ref, wg2_ref, bg_ref,
                        qT_ref, kT_ref, vT_ref, kb_ref, vTb_ref, gq_ref, gk_ref, gv_ref, gr_ref, la_ref):
    y, log_a = _proj_math(x_ref[...], mod_ref[0, 0], mod_ref[0, 1], wm_ref, wl_ref, wg2_ref, bg_ref)
    w = SB_WIDTH
    q = y[:, 0:w] * (SB_HEAD_DIM ** -0.5 * LOG2E)
    k = y[:, w:2 * w]
    v = y[:, 2 * w:3 * w]
    kb_ref[...] = k.astype(BF16)
    qT = q.T.astype(BF16)
    vT = v.T
    kT_ref[0] = k.T
    vT_ref[0] = vT
    vT = vT.astype(BF16)
    for c in range(qT_ref.shape[0]):
        qT_ref[c] = qT[:, c * SB_Q_TILE:(c + 1) * SB_Q_TILE]
    for c in range(vTb_ref.shape[0]):
        vTb_ref[c] = vT[:, c * SB_K_TILE:(c + 1) * SB_K_TILE]
    o = 3 * w
    gq_ref[...] = y[:, o:o + GLA_K_WIDTH]
    gk_ref[...] = y[:, o + GLA_K_WIDTH:o + 2 * GLA_K_WIDTH]
    o += 2 * GLA_K_WIDTH
    gv_ref[...] = y[:, o:o + GLA_V_WIDTH].astype(BF16)
    gr_ref[...] = y[:, o + GLA_V_WIDTH:o + 2 * GLA_V_WIDTH]
    la_ref[...] = log_a


def _proj_sample_kernel(x_ref, mod_ref, wm_ref, wl_ref, wg2_ref, bg_ref,
                        q_ref, k_ref, v_ref, gq_ref, gk_ref, gv_ref, gr_ref, la_ref):
    y, log_a = _proj_math(x_ref[...], mod_ref[0, 0], mod_ref[0, 1], wm_ref, wl_ref, wg2_ref, bg_ref)
    w = SB_WIDTH
    q_ref[...] = y[:, 0:w] * (SB_HEAD_DIM ** -0.5)
    k_ref[...] = y[:, w:2 * w]
    v_ref[...] = y[:, 2 * w:3 * w]
    o = 3 * w
    gq_ref[...] = y[:, o:o + GLA_K_WIDTH]
    gk_ref[...] = y[:, o + GLA_K_WIDTH:o + 2 * GLA_K_WIDTH]
    o += 2 * GLA_K_WIDTH
    gv_ref[...] = y[:, o:o + GLA_V_WIDTH]
    gr_ref[...] = y[:, o + GLA_V_WIDTH:o + 2 * GLA_V_WIDTH]
    la_ref[...] = log_a


def _const_spec(shape):
    nd = len(shape)
    return pl.BlockSpec(shape, lambda *a: (0,) * nd)


def _proj_prompt(x2d, mod, seq_len, wm, wl, wg2, bg):
    n, d = x2d.shape
    tm = min(TOKEN_TILE, seq_len)
    assert seq_len % tm == 0 and tm % SB_Q_TILE == 0
    tiles_per_seq = seq_len // tm
    row = lambda cols: pl.BlockSpec((tm, cols), lambda i: (i, 0))
    dims_major = pl.BlockSpec((1, SB_WIDTH, tm),
                              lambda i: (i // tiles_per_seq, 0, i % tiles_per_seq))
    out_shape = (
        jax.ShapeDtypeStruct((n // SB_Q_TILE, SB_WIDTH, SB_Q_TILE), BF16),
        jax.ShapeDtypeStruct((n // seq_len, SB_WIDTH, seq_len), F32),
        jax.ShapeDtypeStruct((n // seq_len, SB_WIDTH, seq_len), F32),
        jax.ShapeDtypeStruct((n, SB_WIDTH), BF16),
        jax.ShapeDtypeStruct((n // SB_K_TILE, SB_WIDTH, SB_K_TILE), BF16),
        jax.ShapeDtypeStruct((n, GLA_K_WIDTH), F32),
        jax.ShapeDtypeStruct((n, GLA_K_WIDTH), F32),
        jax.ShapeDtypeStruct((n, GLA_V_WIDTH), BF16),
        jax.ShapeDtypeStruct((n, GLA_V_WIDTH), F32),
        jax.ShapeDtypeStruct((n, GLA_K_WIDTH), F32),
    )
    out_specs = (
        pl.BlockSpec((tm // SB_Q_TILE, SB_WIDTH, SB_Q_TILE), lambda i: (i, 0, 0)),
        dims_major, dims_major, row(SB_WIDTH),
        pl.BlockSpec((tm // SB_K_TILE, SB_WIDTH, SB_K_TILE), lambda i: (i, 0, 0)),
        row(GLA_K_WIDTH), row(GLA_K_WIDTH), row(GLA_V_WIDTH), row(GLA_V_WIDTH), row(GLA_K_WIDTH),
    )
    return pl.pallas_call(
        _proj_prompt_kernel,
        grid=(n // tm,),
        in_specs=[row(d),
                  pl.BlockSpec((1, 6, 1, d), lambda i: (i // tiles_per_seq, 0, 0, 0)),
                  _const_spec(wm.shape), _const_spec(wl.shape), _const_spec(wg2.shape),
                  _const_spec(bg.shape)],
        out_specs=out_specs,
        out_shape=out_shape,
        compiler_params=_cparams("parallel"),
        name="proj_prompt",
    )(x2d, mod, wm, wl, wg2, bg)


def _proj_sample(x2d, mod, wm, wl, wg2, bg):
    n, d = x2d.shape
    full = lambda cols: pl.BlockSpec((n, cols), lambda i: (0, 0))
    widths = (SB_WIDTH, SB_WIDTH, SB_WIDTH, GLA_K_WIDTH, GLA_K_WIDTH, GLA_V_WIDTH, GLA_V_WIDTH,
              GLA_K_WIDTH)
    return pl.pallas_call(
        _proj_sample_kernel,
        grid=(1,),
        in_specs=[full(d), _const_spec(mod.shape),
                  _const_spec(wm.shape), _const_spec(wl.shape), _const_spec(wg2.shape),
                  _const_spec(bg.shape)],
        out_specs=tuple(full(c) for c in widths),
        out_shape=tuple(jax.ShapeDtypeStruct((n, c), F32) for c in widths),
        compiler_params=_cparams("arbitrary"),
        name="proj_sample",
    )(x2d, mod, wm, wl, wg2, bg)


def _sb_scan_matrix():
    k = SB_K_TILE
    s = np.arange(k)[:, None]
    j = np.arange(k)[None, :]
    upper = (j > s).astype(np.float32)
    top = np.concatenate([upper, upper], axis=1)
    ones = np.ones((16, 2 * k), np.float32)
    return jnp.asarray(np.concatenate([top, ones], axis=0), dtype=BF16)


def _sb_prompt_kernel(bias_ref, qT_ref, k_ref, vT_ref, a_ref, o_ref):
    hg = pl.program_id(1)
    i = pl.program_id(2)
    tq, tk, hd, nh = SB_Q_TILE, SB_K_TILE, SB_HEAD_DIM, SB_HEADS_PER_STEP
    n_diag = tq // tk
    assert n_diag == 2
    scan = a_ref[...]
    head_row = lax.broadcasted_iota(jnp.int32, (2 * hd, tq), 0)
    key_in_blk = lax.broadcasted_iota(jnp.int32, (tk, tq), 0)
    q_in_tile = lax.broadcasted_iota(jnp.int32, (tk, tq), 1)

    bias_row = lax.broadcasted_iota(jnp.int32, (2 * hd, tq), 0)
    ones_cols = jnp.where(lax.broadcasted_iota(jnp.int32, (tk, 2 * hd), 1) < 3, 1.0, 0.0
                          ).astype(BF16)
    qms = []
    for h in range(nh):
        pair = qT_ref[0, (h // 2) * 2 * hd:(h // 2 + 1) * 2 * hd, :].astype(F32)
        own = (head_row < hd) if h % 2 == 0 else (head_row >= hd)
        parts = _split3(jnp.full((2 * hd, tq), bias_ref[nh * hg + h] * LOG2E, F32))
        bias_tile = jnp.zeros((2 * hd, tq), F32)
        for r, part in enumerate(parts):
            bias_tile = jnp.where(bias_row == r, part.astype(F32), bias_tile)
        qms.append(jnp.concatenate([jnp.where(own, pair, 0.0), bias_tile], axis=0).astype(BF16))

    def logits(kb, h):
        lanes = slice((h // 2) * 2 * hd, (h // 2 + 1) * 2 * hd)
        kblk = k_ref[pl.ds(pl.multiple_of(kb * tk, tk), tk), lanes]
        return _dot(jnp.concatenate([kblk, ones_cols], axis=1), qms[h])

    def key_group(first_kb, state, masked):
        carries, accs = list(state[:nh]), list(state[nh:])
        order = [(u, h) for u in range(n_diag) for h in range(nh)]
        zs, mids, rs, ws = {}, {}, {}, {}

        def stage_logits(c):
            u, h = order[c]
            zs[c] = logits(first_kb + (n_diag - 1 - u), h)

        def stage_scan(c):
            u, h = order[c]
            kb = first_kb + (n_diag - 1 - u)
            cols = slice(tq // 2, tq) if (masked and u == 0) else slice(0, tq)
            z = zs.pop(c)[:, cols]
            soft = jnp.log(1.0 + jnp.exp2(_neg_abs(z))) * LOG2E
            log_beta = jnp.minimum(z, 0.0) - soft
            log_keep = log_beta - z
            valid = None
            if masked:
                valid = ((kb * tk + key_in_blk) < (i * tq + q_in_tile))[:, cols]
                log_keep = jnp.where(valid, log_keep, 0.0)
            hi, lo = _split2(log_keep)
            mids[c] = (log_beta, valid, cols)
            rs[c] = _dot(scan, jnp.concatenate([hi, lo], axis=0))

        def stage_values(c):
            u, h = order[c]
            log_beta, valid, cols = mids.pop(c)
            r = rs.pop(c)
            n_cols = cols.stop - cols.start
            w = jnp.exp2((log_beta + r[:tk]).reshape(tk // 8, 8, n_cols) + carries[h][:, cols]
                         ).reshape(tk, n_cols)
            if masked:
                w = jnp.where(valid, w, 0.0)
            total = r[tk:tk + 8]
            if n_cols < tq:
                w = jnp.concatenate([jnp.zeros((tk, tq - n_cols), F32), w], axis=1)
                total = jnp.concatenate([jnp.zeros((8, tq - n_cols), F32), total], axis=1)
            ws.setdefault(h, []).append(w.astype(BF16))
            carries[h] = carries[h] + total
            if u == n_diag - 1:
                v_cat = jnp.concatenate(
                    [vT_ref[first_kb + (n_diag - 1 - uu), h * hd:(h + 1) * hd, :]
                     for uu in range(n_diag)], axis=1)
                accs[h] = accs[h] + _dot(v_cat, jnp.concatenate(ws.pop(h), axis=0))

        n_chains = len(order)
        for s in range(n_chains + 2 * SB_CHAIN_SKEW):
            if s < n_chains:
                stage_logits(s)
            if 0 <= s - SB_CHAIN_SKEW < n_chains:
                stage_scan(s - SB_CHAIN_SKEW)
            if 0 <= s - 2 * SB_CHAIN_SKEW < n_chains:
                stage_values(s - 2 * SB_CHAIN_SKEW)
        return tuple(carries) + tuple(accs)

    state = tuple([jnp.zeros((8, tq), F32)] * nh + [jnp.zeros((hd, tq), F32)] * nh)
    state = key_group(n_diag * i, state, True)
    state = lax.fori_loop(0, i, lambda t, st: key_group(n_diag * (i - 1 - t), st, False), state)

    oT = jnp.concatenate(state[nh:], axis=0)
    o_ref[...] = oT.T.astype(o_ref.dtype)


def _sb_prompt(bias, qT3, k_bf, vT3, batch, seq_len):
    n = k_bf.shape[0]
    tq, tk = SB_Q_TILE, SB_K_TILE
    assert seq_len % tq == 0
    qt = seq_len // tq
    kblocks = seq_len // tk
    width = SB_HEADS_PER_STEP * SB_HEAD_DIM
    scan = _sb_scan_matrix()
    return pl.pallas_call(
        _sb_prompt_kernel,
        grid=(batch, SB_HEADS // SB_HEADS_PER_STEP, qt),
        in_specs=[pl.BlockSpec(memory_space=pltpu.SMEM),
                  pl.BlockSpec((1, width, tq), lambda b, hg, i: (b * qt + i, hg, 0)),
                  pl.BlockSpec((seq_len, width), lambda b, hg, i: (b, hg)),
                  pl.BlockSpec((kblocks, width, tk), lambda b, hg, i: (b, hg, 0)),
                  _const_spec(scan.shape)],
        out_specs=pl.BlockSpec((tq, width), lambda b, hg, i: (b * qt + i, hg)),
        out_shape=jax.ShapeDtypeStruct((n, SB_WIDTH), BF16),
        compiler_params=_cparams("parallel", "parallel", "arbitrary"),
        name="sb_prompt",
    )(bias, qT3, k_bf, vT3, scan)


def _gla_consts():
    t = np.arange(GLA_BLOCK)
    same = (t[:, None] // GLA_CHUNK) == (t[None, :] // GLA_CHUNK)
    tri = (same & (t[None, :] <= t[:, None])).astype(np.float32)
    ones = same.astype(np.float32)
    scan = np.concatenate([np.concatenate([tri] * 3, axis=1),
                           np.concatenate([ones] * 3, axis=1)], axis=0)
    n_chunks = GLA_BLOCK // GLA_CHUNK
    ind = np.zeros((GLA_BLOCK, n_chunks * LANES), np.float32)
    for c in range(n_chunks):
        ind[c * GLA_CHUNK:(c + 1) * GLA_CHUNK, c * LANES:(c + 1) * LANES] = 1.0
    ind = np.concatenate([ind] * 3, axis=0)
    return jnp.asarray(scan, dtype=BF16), jnp.asarray(ind, dtype=BF16)


def _gla_prompt_kernel(gq_ref, gk_ref, gv_ref, la_ref, gr_ref, ng_ref, scan_ref, ind_ref,
                       og_ref, st_ref, s_scr):
    j = pl.program_id(1)
    tb, c_len = GLA_BLOCK, GLA_CHUNK
    n_chunks = tb // c_len
    dk, dv, nh = GLA_HEAD_K, GLA_HEAD_V, GLA_HEADS

    @pl.when(j == 0)
    def _():
        s_scr[...] = jnp.zeros_like(s_scr)

    la = la_ref[...]
    parts = jnp.concatenate(_split3(la), axis=0)
    bb = _dot(scan_ref[...], parts)
    b = bb[:tb]
    b_last = bb[tb:]
    laT_parts = jnp.concatenate(_split3(la.T), axis=1)
    b_last_col = _dot(laT_parts, ind_ref[...])

    q = gq_ref[...] * (dk ** -0.5)
    k = gk_ref[...]
    v = gv_ref[...]
    q_dec = q * jnp.exp(b)
    k_inv = (k * jnp.exp(-b)).astype(BF16)
    k_endT = (k * jnp.exp(b_last - b)).T

    lane_head = _div_pow2(lax.broadcasted_iota(jnp.int32, (tb, nh * dk), 1), dk)
    ti = lax.broadcasted_iota(jnp.int32, (tb, tb), 0)
    tj = lax.broadcasted_iota(jnp.int32, (tb, tb), 1)
    tj_chunk = _div_pow2(tj, c_len)
    causal = (_div_pow2(ti, c_len) == tj_chunk) & (tj <= ti)

    qm = [jnp.where(lane_head == h, q_dec, 0.0).astype(BF16) for h in range(nh)]
    scores = [_dot_nt(qm[h], k_inv) for h in range(nh)]
    us = []
    for c in range(n_chunks):
        k_endT_c = jnp.where(tj_chunk == c, k_endT, 0.0).astype(BF16)
        us.append(_dot(k_endT_c, v))
    o_intra = [_dot(jnp.where(causal, scores[h], 0.0).astype(BF16), v[:, h * dv:(h + 1) * dv])
               for h in range(nh)]

    states = [s_scr[...]]
    for c in range(n_chunks):
        u_diag = jnp.concatenate(
            [us[c][h * dk:(h + 1) * dk, h * dv:(h + 1) * dv] for h in range(nh)], axis=0)
        states.append(jnp.exp(b_last_col[:, c * LANES:(c + 1) * LANES]) * states[c] + u_diag)
    o_inter = []
    for c in range(n_chunks):
        rows = slice(c * c_len, (c + 1) * c_len)
        q_stack = jnp.concatenate([qm[h][rows] for h in range(nh)], axis=0)
        o_inter.append(_dot(q_stack, states[c].astype(BF16)))

    norm_g = ng_ref[...]
    for c in range(n_chunks):
        rows = slice(c * c_len, (c + 1) * c_len)
        for h in range(nh):
            o = o_intra[h][rows] + o_inter[c][h * c_len:(h + 1) * c_len]
            o = o * lax.rsqrt(jnp.mean(o * o, axis=-1, keepdims=True) + GN_EPS) * norm_g
            r = gr_ref[rows, h * dv:(h + 1) * dv]
            og_ref[rows, h * dv:(h + 1) * dv] = (o * (r * _sigmoid(r))).astype(og_ref.dtype)
    s_all = states[n_chunks]
    s_scr[...] = s_all

    @pl.when(j == pl.num_programs(1) - 1)
    def _():
        st_ref[0] = s_all


def _gla_prompt(gq, gk, gv, la, gr, norm_g, batch, seq_len):
    n = gq.shape[0]
    tb = GLA_BLOCK
    assert seq_len % tb == 0
    nb = seq_len // tb
    scan, ind = _gla_consts()
    row = lambda cols: pl.BlockSpec((tb, cols), lambda b, j: (b * nb + j, 0))
    rows_s = GLA_HEADS * GLA_HEAD_K
    return pl.pallas_call(
        _gla_prompt_kernel,
        grid=(batch, nb),
        in_specs=[row(GLA_K_WIDTH), row(GLA_K_WIDTH), row(GLA_V_WIDTH), row(GLA_K_WIDTH),
                  row(GLA_V_WIDTH), _const_spec((1, GLA_HEAD_V)),
                  _const_spec(scan.shape), _const_spec(ind.shape)],
        out_specs=(row(GLA_V_WIDTH),
                   pl.BlockSpec((1, rows_s, GLA_HEAD_V), lambda b, j: (b, 0, 0))),
        out_shape=(jax.ShapeDtypeStruct((n, GLA_V_WIDTH), BF16),
                   jax.ShapeDtypeStruct((batch, rows_s, GLA_HEAD_V), F32)),
        scratch_shapes=[pltpu.VMEM((rows_s, GLA_HEAD_V), F32)],
        compiler_params=_cparams("parallel", "arbitrary"),
        name="gla_prompt",
    )(gq, gk, gv, la, gr, norm_g.reshape(1, GLA_HEAD_V), scan, ind)


POST_SEGMENTS = 8


def _post_math(alpha, x_ref, osb_ref, og_ref, mod_ref, wg_ref, wua_ref, wub_ref, wo_ref,
               ln1g_ref, ln1b_ref, wf1_ref, wf2_ref, ln2g_ref, ln2b_ref, y_ref, segment):
    x = x_ref[...]
    sh1, sc1, gt1, sh2, sc2, gt2 = [mod_ref[0, m] for m in range(6)]
    d = x.shape[-1]
    assert sum(FF_CHUNKS) == wf1_ref.shape[1] and POST_SEGMENTS == 3 + len(FF_CHUNKS)

    def seg0():
        h = (_ln_stats(x) * (1.0 + sc1) + sh1).astype(BF16)
        return h, _sigmoid(_dot(h, wg_ref[:, :d]))

    h, gate_a = segment(0, seg0)
    part_a, gate_b = segment(1, lambda: (gate_a * _dot(osb_ref[...].astype(BF16), wua_ref[...]),
                                         _sigmoid(_dot(h, wg_ref[:, d:]))))

    def seg2():
        merged = (part_a + gate_b * _dot(og_ref[...].astype(BF16), wub_ref[...])).astype(BF16)
        x1 = _ln_stats(alpha * x + gt1 * _dot(merged, wo_ref[...])) * ln1g_ref[...] + ln1b_ref[...]
        return x1, (_ln_stats(x1) * (1.0 + sc2) + sh2).astype(BF16)

    x1, h2 = segment(2, seg2)
    f = jnp.zeros_like(x)
    lo = 0
    for c, width in enumerate(FF_CHUNKS):
        cols = slice(lo, lo + width)
        lo += width

        def ff(f=f, cols=cols, last=(c == len(FF_CHUNKS) - 1)):
            u = jnp.maximum(_dot(h2, wf1_ref[:, cols]), 0.0)
            f_new = f + _dot((u * u).astype(BF16), wf2_ref[cols, :])
            if last:
                y_ref[...] = _ln_stats(alpha * x1 + gt2 * f_new) * ln2g_ref[...] + ln2b_ref[...]
            return f_new

        f = segment(3 + c, ff)


def _post_kernel(alpha, *refs):
    _post_math(alpha, *refs, segment=lambda j, fn: fn())


def _post(x2d, o_sb, o_g, mod, seq_len, alpha, wg, wua, wub, wo, ln1g, ln1b, wf1, wf2, ln2g, ln2b):
    n, d = x2d.shape
    per_token = mod.shape[2] != 1
    tm = n if per_token else min(TOKEN_TILE, seq_len)
    tiles_per_seq = 1 if per_token else seq_len // tm
    row = lambda cols: pl.BlockSpec((tm, cols), lambda i: (i, 0))
    single = lambda a: pl.BlockSpec(a.shape, lambda i: (0,) * a.ndim, pipeline_mode=pl.Buffered(1))
    vec = lambda a: a.reshape(1, d)
    mod_spec = (_const_spec(mod.shape) if per_token else
                pl.BlockSpec((1, 6, 1, d), lambda i: (i // tiles_per_seq, 0, 0, 0)))
    return pl.pallas_call(
        functools.partial(_post_kernel, alpha),
        grid=(n // tm,),
        in_specs=[row(d), row(SB_WIDTH), row(GLA_V_WIDTH), mod_spec,
                  single(wg), single(wua), single(wub), single(wo),
                  _const_spec((1, d)), _const_spec((1, d)),
                  single(wf1), single(wf2),
                  _const_spec((1, d)), _const_spec((1, d))],
        out_specs=row(d),
        out_shape=jax.ShapeDtypeStruct((n, d), F32),
        compiler_params=_cparams("parallel"),
        name="post",
    )(x2d, o_sb, o_g, mod, wg, wua, wub, wo, vec(ln1g), vec(ln1b), wf1, wf2, vec(ln2g), vec(ln2b))


def _lane_scan_matrix():
    j = np.arange(LANES)[:, None]
    s = np.arange(LANES)[None, :]
    later = (j > s).astype(np.float32)
    half = np.concatenate([later, np.ones((LANES, LANES), np.float32)], axis=1)
    return jnp.asarray(np.concatenate([half, half], axis=0), dtype=BF16)


def _paged_logits(k_pages, qb_scr, bias_ref, lb_scr, hl_scr):
    p, nh, hd = PAGED_PAGES, SB_HEADS, SB_HEAD_DIM
    sublane = lax.broadcasted_iota(jnp.int32, (nh, LANES), 0)
    zs = [jnp.zeros((nh, LANES), F32)] * p
    bias = jnp.zeros((nh, LANES), F32)
    for h in range(nh):
        hrows = slice(h * hd, (h + 1) * hd)
        qh = qb_scr[hrows, :]
        bias = jnp.where(sublane == h, bias_ref[h], bias)
        for s in range(p):
            tot = jnp.sum(k_pages[s, hrows, :] * qh, axis=0, keepdims=True)
            zs[s] = jnp.where(sublane == h, tot, zs[s])
    z = jnp.concatenate([zs[s] + bias for s in range(p)], axis=0)
    soft = jnp.log1p(jnp.exp(-jnp.abs(z)))
    log_beta = jnp.minimum(z, 0.0) - soft
    hi, lo = _split2(log_beta - z)
    lb_scr[...] = log_beta
    hl_scr[...] = jnp.concatenate([hi, lo], axis=1)


def _paged_accumulate(v_pages, r1, first, lb_scr, carry_scr, acc_scr):
    p, nh, hd = PAGED_PAGES, SB_HEADS, SB_HEAD_DIM
    carry = jnp.where(first, 0.0, carry_scr[...])
    row_tot = r1[:, LANES:]
    prefix, run = [], carry
    for s in range(p):
        prefix.append(run)
        run = run + row_tot[s * nh:(s + 1) * nh]
    carry_scr[...] = run
    w = jnp.exp(lb_scr[...] + r1[:, :LANES] + jnp.concatenate(prefix, axis=0))
    for h in range(nh):
        hrows = slice(h * hd, (h + 1) * hd)
        acc = jnp.where(first, 0.0, acc_scr[hrows, :])
        for s in range(p):
            acc = acc + v_pages[s, hrows, :] * w[s * nh + h:s * nh + h + 1, :]
        acc_scr[hrows, :] = acc


def _post_paged_kernel(alpha, n_pages, pt_ref, bias_ref, *refs):
    post_refs = refs[:14]
    q_ref, lscan_ref, kc_hbm, vc_hbm = refs[14:18]
    y_ref, os_ref = refs[18:20]
    kbuf, vbuf, ksem, vsem, carry_scr, acc_scr, qb_scr, lb_scr, hl_scr = refs[20:]
    p, subs, slots = PAGED_PAGES, POST_SEGMENTS, PAGE_SLOTS
    groups = n_pages // p
    i = pl.program_id(0)
    n_steps = pl.num_programs(0)

    def slot_of(t):
        return lax.rem(t + slots, slots)

    def copies(hbm, buf, sem, t):
        b, g, slot = lax.div(t, groups), lax.rem(t, groups), slot_of(t)
        return [pltpu.make_async_copy(hbm.at[pt_ref[b, n_pages - 1 - (g * p + s)]],
                                      buf.at[slot, s], sem.at[slot]) for s in range(p)]

    def start(cs):
        for c in cs:
            c.start()

    def wait(cs):
        for c in cs:
            c.wait()

    def finish_sequence(b):
        acc_t = acc_scr[...].T
        os_ref[pl.ds(b, 1), :] = jnp.sum(acc_t, axis=0, keepdims=True)

    scanned = {}

    def before(j):
        t = subs * i + j
        if j == 0:
            @pl.when(i == 0)
            def _():
                hl_scr[...] = jnp.zeros_like(hl_scr)
                lb_scr[...] = jnp.zeros_like(lb_scr)
                carry_scr[...] = jnp.zeros_like(carry_scr)
                acc_scr[...] = jnp.zeros_like(acc_scr)
                vbuf[slots - 1] = jnp.zeros(vbuf.shape[1:], vbuf.dtype)
                start(copies(kc_hbm, kbuf, ksem, t))
                start(copies(kc_hbm, kbuf, ksem, t + 1))
                start(copies(vc_hbm, vbuf, vsem, t))

            @pl.when(i > 0)
            def _():
                wait(copies(vc_hbm, vbuf, vsem, t - 1))
        else:
            wait(copies(vc_hbm, vbuf, vsem, t - 1))
        wait(copies(kc_hbm, kbuf, ksem, t))
        for ahead, hbm, buf, sem in ((2, kc_hbm, kbuf, ksem), (1, vc_hbm, vbuf, vsem)):
            if j + ahead < subs:
                start(copies(hbm, buf, sem, t + ahead))
            else:
                @pl.when(i + 1 < n_steps)
                def _(ahead=ahead, hbm=hbm, buf=buf, sem=sem):
                    start(copies(hbm, buf, sem, t + ahead))
        scanned[j] = _dot(hl_scr[...], lscan_ref[...])

    def after(j):
        t = subs * i + j
        first = lax.rem(t - 1 + groups, groups) == 0
        _paged_accumulate(vbuf.at[slot_of(t - 1)], scanned[j], first, lb_scr, carry_scr, acc_scr)
        if j == 0:
            @pl.when(jnp.logical_and(lax.rem(t, groups) == 0, t > 0))
            def _():
                finish_sequence(lax.div(t, groups) - 1)

            @pl.when(lax.rem(t, groups) == 0)
            def _():
                q_row = q_ref[pl.ds(lax.div(t, groups), 1), :]
                qb_scr[...] = jnp.broadcast_to(q_row, (LANES, q_row.shape[1])).T
        _paged_logits(kbuf.at[slot_of(t)], qb_scr, bias_ref, lb_scr, hl_scr)

    def segment(j, fn):
        before(j)
        out = fn()
        after(j)
        return out

    _post_math(alpha, *post_refs, y_ref, segment=segment)

    @pl.when(i == n_steps - 1)
    def _():
        t_last = subs * n_steps - 1
        wait(copies(vc_hbm, vbuf, vsem, t_last))
        r1 = _dot(hl_scr[...], lscan_ref[...])
        _paged_accumulate(vbuf.at[slot_of(t_last)], r1, False, lb_scr, carry_scr, acc_scr)
        finish_sequence(lax.div(t_last, groups))


def _post_with_paged_attention(x2d, o_sb, o_g, mod, seq_len, alpha, post_w,
                               page_table, bias, q_s, cache_kT, cache_vT):
    wg, wua, wub, wo, ln1g, ln1b, wf1, wf2, ln2g, ln2b = post_w
    n, d = x2d.shape
    nseq, n_pages = page_table.shape
    p, width = PAGED_PAGES, SB_WIDTH
    tm = min(TOKEN_TILE, seq_len)
    tiles_per_seq = seq_len // tm
    steps = n // tm
    assert cache_kT.shape[1:] == (width, LANES) and n_pages % p == 0
    assert (n_pages // p) % POST_SEGMENTS == 0 and steps * POST_SEGMENTS * p == nseq * n_pages
    lscan = _lane_scan_matrix()

    row = lambda cols: pl.BlockSpec((tm, cols), lambda i, pt: (i, 0))
    const = lambda shape: pl.BlockSpec(shape, lambda i, pt: (0,) * len(shape))
    single = lambda a: pl.BlockSpec(a.shape, lambda i, pt: (0,) * a.ndim,
                                    pipeline_mode=pl.Buffered(1))
    vec = lambda a: a.reshape(1, d)
    hbm = pl.BlockSpec(memory_space=pl.ANY)
    grid_spec = pltpu.PrefetchScalarGridSpec(
        num_scalar_prefetch=1,
        grid=(steps,),
        in_specs=[pl.BlockSpec(memory_space=pltpu.SMEM),
                  row(d), row(SB_WIDTH), row(GLA_V_WIDTH),
                  pl.BlockSpec((1, 6, 1, d), lambda i, pt: (i // tiles_per_seq, 0, 0, 0)),
                  single(wg), single(wua), single(wub), single(wo),
                  const((1, d)), const((1, d)), single(wf1), single(wf2),
                  const((1, d)), const((1, d)),
                  const(q_s.shape), const(lscan.shape), hbm, hbm],
        out_specs=(row(d), const((nseq, width))),
        scratch_shapes=[pltpu.VMEM((PAGE_SLOTS, p, width, LANES), F32),
                        pltpu.VMEM((PAGE_SLOTS, p, width, LANES), F32),
                        pltpu.SemaphoreType.DMA((PAGE_SLOTS,)),
                        pltpu.SemaphoreType.DMA((PAGE_SLOTS,)),
                        pltpu.VMEM((SB_HEADS, LANES), F32),
                        pltpu.VMEM((width, LANES), F32),
                        pltpu.VMEM((width, LANES), F32),
                        pltpu.VMEM((p * SB_HEADS, LANES), F32),
                        pltpu.VMEM((p * SB_HEADS, 2 * LANES), BF16)],
    )
    return pl.pallas_call(
        functools.partial(_post_paged_kernel, alpha, n_pages),
        grid_spec=grid_spec,
        out_shape=(jax.ShapeDtypeStruct((n, d), F32), jax.ShapeDtypeStruct((nseq, width), F32)),
        compiler_params=_cparams("arbitrary"),
        name="post_paged",
    )(page_table, bias, x2d, o_sb, o_g, mod, wg, wua, wub, wo, vec(ln1g), vec(ln1b), wf1, wf2,
      vec(ln2g), vec(ln2b), q_s, lscan, cache_kT, cache_vT)


def _gla_sample_kernel(gq_ref, gk_ref, la_ref, gv_ref, gr_ref, ng_ref, s0_ref, og_ref, st_ref):
    nh, dk, dv = GLA_HEADS, GLA_HEAD_K, GLA_HEAD_V
    la = la_ref[0]
    q_dec = gq_ref[0] * (dk ** -0.5) * jnp.exp(la)
    k_inv = gk_ref[0] * jnp.exp(-la)
    k_end = gk_ref[0]
    s0 = s0_ref[0]
    v8 = gv_ref[0]
    qk = q_dec * k_inv
    qs = q_dec * s0
    outs = []
    for h in range(nh):
        rows = slice(h * dk, (h + 1) * dk)
        att = jnp.sum(qk[rows], axis=0, keepdims=True)
        outs.append(att * v8[h:h + 1] + jnp.sum(qs[rows], axis=0, keepdims=True))
    o = jnp.concatenate(outs + [jnp.zeros((8 - nh, dv), F32)], axis=0)
    o = o * lax.rsqrt(jnp.mean(o * o, axis=-1, keepdims=True) + GN_EPS) * ng_ref[...]
    r = gr_ref[0]
    og_ref[0] = o * (r * _sigmoid(r))
    v_rows = jnp.concatenate(
        [jnp.broadcast_to(v8[h:h + 1], (dk, dv)) for h in range(nh)], axis=0)
    st_ref[0] = jnp.exp(la) * s0 + k_end * v_rows


def _gla_sample(gq, gk, gv, la, gr, norm_g, s0):
    n = gq.shape[0]
    nh, dk, dv = GLA_HEADS, GLA_HEAD_K, GLA_HEAD_V
    kw = nh * dk
    pad_heads = lambda a: jnp.pad(a.reshape(n, nh, dv), ((0, 0), (0, 8 - nh), (0, 0)))
    colspec = pl.BlockSpec((1, kw, 1), lambda b: (b, 0, 0))
    headspec = pl.BlockSpec((1, 8, dv), lambda b: (b, 0, 0))
    statespec = pl.BlockSpec((1, kw, dv), lambda b: (b, 0, 0))
    og, st = pl.pallas_call(
        _gla_sample_kernel,
        grid=(n,),
        in_specs=[colspec, colspec, colspec, headspec, headspec, _const_spec((1, dv)), statespec],
        out_specs=(headspec, statespec),
        out_shape=(jax.ShapeDtypeStruct((n, 8, dv), F32),
                   jax.ShapeDtypeStruct((n, kw, dv), F32)),
        compiler_params=_cparams("parallel"),
        name="gla_sample",
    )(gq.reshape(n, kw, 1), gk.reshape(n, kw, 1), la.reshape(n, kw, 1),
      pad_heads(gv), pad_heads(gr), norm_g.reshape(1, dv), s0)
    return og[:, :nh].reshape(n, nh * dv), st


def kernel(x_prompt, x_sample, cache_k, cache_v, state_gla, page_table, c_prompt, c_sample,
           w_ada, b_ada, w_in, sb_bias, w_gla_g2, b_gla_g, gla_norm_g, w_up_a, w_up_b, w_o,
           ln1_g, ln1_b, w_ff1, w_ff2, ln2_g, ln2_b):
    depth = w_ada.shape[0]
    pb, seq_len, d = x_prompt.shape
    sb_n = x_sample.shape[0]
    assert x_sample.shape[1] == 1
    alpha = (2.0 * depth) ** 0.25
    n_phys, page_size = cache_k.shape[1], cache_k.shape[2]

    yp = x_prompt.reshape(pb * seq_len, d)
    ys = x_sample.reshape(sb_n, d)
    c_all = jnp.concatenate([c_prompt, c_sample], axis=0)
    main_w = 3 * SB_WIDTH + 2 * GLA_K_WIDTH + 2 * GLA_V_WIDTH
    gate_off = main_w + GLA_GATE_RANK

    outs = [[] for _ in range(6)]
    for l in range(depth):
        ada = _ada(c_all, w_ada[l].astype(BF16), b_ada[l])
        mod_p = ada[:pb].reshape(pb, 6, 1, d)
        mod_s = ada[pb:].reshape(sb_n, 6, d).transpose(1, 0, 2).reshape(1, 6, sb_n, d)

        w_in_l = w_in[l]
        wm = w_in_l[:, :main_w].astype(BF16)
        wl = jnp.pad(w_in_l[:, main_w:gate_off], ((0, 0), (0, LANES - GLA_GATE_RANK))).astype(BF16)
        wg = w_in_l[:, gate_off:].astype(BF16)
        wg2 = jnp.pad(w_gla_g2[l], ((0, LANES - GLA_GATE_RANK), (0, 0))).astype(BF16)
        bg = b_gla_g[l].reshape(1, GLA_K_WIDTH)
        post_w = (wg, w_up_a[l].astype(BF16), w_up_b[l].astype(BF16), w_o[l].astype(BF16),
                  ln1_g[l], ln1_b[l], w_ff1[l].astype(BF16), w_ff2[l].astype(BF16),
                  ln2_g[l], ln2_b[l])

        qT3, k_p, v_p, k_bf, vT3, gq, gk, gv, gr, la = _proj_prompt(yp, mod_p, seq_len, wm, wl, wg2, bg)
        q_s, k_s, v_s, gq_s, gk_s, gv_s, gr_s, la_s = _proj_sample(ys, mod_s, wm, wl, wg2, bg)
        o_sb = _sb_prompt(sb_bias[l], qT3, k_bf, vT3, pb, seq_len)
        o_g, st_p = _gla_prompt(gq, gk, gv, la, gr, gla_norm_g[l], pb, seq_len)

        page_major = lambda c: jnp.transpose(c, (0, 2, 3, 1)).reshape(n_phys, SB_WIDTH, page_size)
        yp, o_sb_s = _post_with_paged_attention(
            yp, o_sb, o_g, mod_p, seq_len, alpha, post_w, page_table, sb_bias[l], q_s,
            page_major(cache_k[l]), page_major(cache_v[l]))

        s0 = state_gla[l].reshape(sb_n, GLA_HEADS * GLA_HEAD_K, GLA_HEAD_V)
        o_g_s, st_s = _gla_sample(gq_s, gk_s, gv_s, la_s, gr_s, gla_norm_g[l], s0)
        ys = _post(ys, o_sb_s, o_g_s, mod_s, 1, alpha, *post_w)

        rows_major = lambda t: jnp.transpose(
            t.reshape(pb, SB_HEADS, SB_HEAD_DIM, seq_len), (0, 3, 1, 2))
        outs[0].append(rows_major(k_p))
        outs[1].append(rows_major(v_p))
        outs[2].append(st_p.reshape(pb, GLA_HEADS, GLA_HEAD_K, GLA_HEAD_V))
        outs[3].append(k_s.reshape(sb_n, 1, SB_HEADS, SB_HEAD_DIM))
        outs[4].append(v_s.reshape(sb_n, 1, SB_HEADS, SB_HEAD_DIM))
        outs[5].append(st_s.reshape(sb_n, GLA_HEADS, GLA_HEAD_K, GLA_HEAD_V))

    stacked = [jnp.stack(o) for o in outs]
    return (yp.reshape(pb, seq_len, d), ys.reshape(sb_n, 1, d),
            stacked[0], stacked[1], stacked[2].astype(state_gla.dtype),
            stacked[3], stacked[4], stacked[5].astype(state_gla.dtype))
```

```python
import functools

import numpy as np
import jax
import jax.numpy as jnp
from jax import lax
from jax.experimental import pallas as pl
from jax.experimental.pallas import tpu as pltpu

F32 = jnp.float32
BF16 = jnp.bfloat16

SB_HEADS = 8
SB_HEAD_DIM = 64
SB_WIDTH = SB_HEADS * SB_HEAD_DIM
GLA_HEADS = 4
GLA_HEAD_K = 64
GLA_HEAD_V = 128
GLA_K_WIDTH = GLA_HEADS * GLA_HEAD_K
GLA_V_WIDTH = GLA_HEADS * GLA_HEAD_V
GLA_GATE_RANK = 16
GLA_TAU = 16.0
GLA_CHUNK = 64
LN_EPS = 1e-5
GN_EPS = 1e-6

LANES = 128
VMEM_LIMIT_BYTES = 56 * 1024 * 1024

TOKEN_TILE = 512
SB_Q_TILE = 256
SB_K_TILE = 128
SB_CHAIN_SKEW = 8
SB_HEADS_PER_STEP = 8
GLA_BLOCK = 256
PAGED_PAGES = 8
FF_CHUNKS = (768, 768, 768, 768, 1024)
PAGE_SLOTS = 3


def _cparams(*sem):
    return pltpu.CompilerParams(dimension_semantics=sem, vmem_limit_bytes=VMEM_LIMIT_BYTES)


def _dot(a, b):
    return jnp.dot(a, b, preferred_element_type=F32)


def _dot_nt(a, b):
    return lax.dot_general(a, b, (((1,), (1,)), ((), ())), preferred_element_type=F32)


def _ln_stats(x):
    mu = jnp.mean(x, axis=-1, keepdims=True)
    xc = x - mu
    var = jnp.mean(xc * xc, axis=-1, keepdims=True)
    return xc * lax.rsqrt(var + LN_EPS)


def _log_sigmoid(x):
    return jnp.minimum(x, 0.0) - jnp.log1p(jnp.exp(-jnp.abs(x)))


def _sigmoid(x):
    return 1.0 / (1.0 + jnp.exp(-x))


LOG2E = 1.4426950408889634


def _neg_abs(x):
    bits = lax.bitcast_convert_type(x, jnp.int32) | jnp.int32(-2 ** 31)
    return lax.bitcast_convert_type(bits, F32)


def _div_pow2(x, n):
    assert n & (n - 1) == 0
    return jnp.right_shift(x, n.bit_length() - 1)


def _split2(x):
    hi = x.astype(BF16)
    lo = (x - hi.astype(F32)).astype(BF16)
    return hi, lo


def _split3(x):
    hi = x.astype(BF16)
    r = x - hi.astype(F32)
    mid = r.astype(BF16)
    lo = (r - mid.astype(F32)).astype(BF16)
    return hi, mid, lo


def _ada_kernel(c_ref, w_ref, b_ref, o_ref):
    c = c_ref[...]
    a = c * _sigmoid(c)
    o_ref[...] = _dot(a.astype(BF16), w_ref[...]) + b_ref[...]


def _ada(c_all, w_ada_b, b_ada):
    n, d = c_all.shape
    nout = w_ada_b.shape[1]
    blk = d
    return pl.pallas_call(
        _ada_kernel,
        grid=(nout // blk,),
        in_specs=[pl.BlockSpec((n, d), lambda j: (0, 0)),
                  pl.BlockSpec((d, blk), lambda j: (0, j)),
                  pl.BlockSpec((1, blk), lambda j: (0, j))],
        out_specs=pl.BlockSpec((n, blk), lambda j: (0, j)),
        out_shape=jax.ShapeDtypeStruct((n, nout), F32),
        compiler_params=_cparams("parallel"),
        name="ada",
    )(c_all, w_ada_b, b_ada.reshape(1, nout))


def _proj_math(x, sh1, sc1, wm_ref, wl_ref, wg2_ref, bg_ref):
    h = (_ln_stats(x) * (1.0 + sc1) + sh1).astype(BF16)
    y = _dot(h, wm_ref[...])
    g_low = _dot(h, wl_ref[...])
    gl = _dot(g_low.astype(BF16), wg2_ref[...]) + bg_ref[...]
    log_a = _log_sigmoid(gl) * (1.0 / GLA_TAU)
    return y, log_a


def _proj_prompt_kernel(x_ref, mod_ref, wm_ref, wl_ref, wg2_ref, bg_ref,
                        qT_ref, kT_ref, vT_ref, kb_ref, vTb_ref, gq_ref, gk_ref, gv_ref, gr_ref, la_ref):
    y, log_a = _proj_math(x_ref[...], mod_ref[0, 0], mod_ref[0, 1], wm_ref, wl_ref, wg2_ref, bg_ref)
    w = SB_WIDTH
    q = y[:, 0:w] * (SB_HEAD_DIM ** -0.5 * LOG2E)
    k = y[:, w:2 * w]
    v = y[:, 2 * w:3 * w]
    kb_ref[...] = k.astype(BF16)
    qT = q.T.astype(BF16)
    vT = v.T
    kT_ref[0] = k.T
    vT_ref[0] = vT
    vT = vT.astype(BF16)
    for c in range(qT_ref.shape[0]):
        qT_ref[c] = qT[:, c * SB_Q_TILE:(c + 1) * SB_Q_TILE]
    for c in range(vTb_ref.shape[0]):
        vTb_ref[c] = vT[:, c * SB_K_TILE:(c + 1) * SB_K_TILE]
    o = 3 * w
    gq_ref[...] = y[:, o:o + GLA_K_WIDTH]
    gk_ref[...] = y[:, o + GLA_K_WIDTH:o + 2 * GLA_K_WIDTH]
    o += 2 * GLA_K_WIDTH
    gv_ref[...] = y[:, o:o + GLA_V_WIDTH].astype(BF16)
    gr_ref[...] = y[:, o + GLA_V_WIDTH:o + 2 * GLA_V_WIDTH]
    la_ref[...] = log_a


def _proj_sample_kernel(x_ref, mod_ref, wm_ref, wl_ref, wg2_ref, bg_ref,
                        q_ref, k_ref, v_ref, gq_ref, gk_ref, gv_ref, gr_ref, la_ref):
    y, log_a = _proj_math(x_ref[...], mod_ref[0, 0], mod_ref[0, 1], wm_ref, wl_ref, wg2_ref, bg_ref)
    w = SB_WIDTH
    q_ref[...] = y[:, 0:w] * (SB_HEAD_DIM ** -0.5)
    k_ref[...] = y[:, w:2 * w]
    v_ref[...] = y[:, 2 * w:3 * w]
    o = 3 * w
    gq_ref[...] = y[:, o:o + GLA_K_WIDTH]
    gk_ref[...] = y[:, o + GLA_K_WIDTH:o + 2 * GLA_K_WIDTH]
    o += 2 * GLA_K_WIDTH
    gv_ref[...] = y[:, o:o + GLA_V_WIDTH]
    gr_ref[...] = y[:, o + GLA_V_WIDTH:o + 2 * GLA_V_WIDTH]
    la_ref[...] = log_a


def _const_spec(shape):
    nd = len(shape)
    return pl.BlockSpec(shape, lambda *a: (0,) * nd)


def _proj_prompt(x2d, mod, seq_len, wm, wl, wg2, bg):
    n, d = x2d.shape
    tm = min(TOKEN_TILE, seq_len)
    assert seq_len % tm == 0 and tm % SB_Q_TILE == 0
    tiles_per_seq = seq_len // tm
    row = lambda cols: pl.BlockSpec((tm, cols), lambda i: (i, 0))
    dims_major = pl.BlockSpec((1, SB_WIDTH, tm),
                              lambda i: (i // tiles_per_seq, 0, i % tiles_per_seq))
    out_shape = (
        jax.ShapeDtypeStruct((n // SB_Q_TILE, SB_WIDTH, SB_Q_TILE), BF16),
        jax.ShapeDtypeStruct((n // seq_len, SB_WIDTH, seq_len), F32),
        jax.ShapeDtypeStruct((n // seq_len, SB_WIDTH, seq_len), F32),
        jax.ShapeDtypeStruct((n, SB_WIDTH), BF16),
        jax.ShapeDtypeStruct((n // SB_K_TILE, SB_WIDTH, SB_K_TILE), BF16),
        jax.ShapeDtypeStruct((n, GLA_K_WIDTH), F32),
        jax.ShapeDtypeStruct((n, GLA_K_WIDTH), F32),
        jax.ShapeDtypeStruct((n, GLA_V_WIDTH), BF16),
        jax.ShapeDtypeStruct((n, GLA_V_WIDTH), F32),
        jax.ShapeDtypeStruct((n, GLA_K_WIDTH), F32),
    )
    out_specs = (
        pl.BlockSpec((tm // SB_Q_TILE, SB_WIDTH, SB_Q_TILE), lambda i: (i, 0, 0)),
        dims_major, dims_major, row(SB_WIDTH),
        pl.BlockSpec((tm // SB_K_TILE, SB_WIDTH, SB_K_TILE), lambda i: (i, 0, 0)),
        row(GLA_K_WIDTH), row(GLA_K_WIDTH), row(GLA_V_WIDTH), row(GLA_V_WIDTH), row(GLA_K_WIDTH),
    )
    return pl.pallas_call(
        _proj_prompt_kernel,
        grid=(n // tm,),
        in_specs=[row(d),
                  pl.BlockSpec((1, 6, 1, d), lambda i: (i // tiles_per_seq, 0, 0, 0)),
                  _const_spec(wm.shape), _const_spec(wl.shape), _const_spec(wg2.shape),
                  _const_spec(bg.shape)],
        out_specs=out_specs,
        out_shape=out_shape,
        compiler_params=_cparams("parallel"),
        name="proj_prompt",
    )(x2d, mod, wm, wl, wg2, bg)


def _proj_sample(x2d, mod, wm, wl, wg2, bg):
    n, d = x2d.shape
    full = lambda cols: pl.BlockSpec((n, cols), lambda i: (0, 0))
    widths = (SB_WIDTH, SB_WIDTH, SB_WIDTH, GLA_K_WIDTH, GLA_K_WIDTH, GLA_V_WIDTH, GLA_V_WIDTH,
              GLA_K_WIDTH)
    return pl.pallas_call(
        _proj_sample_kernel,
        grid=(1,),
        in_specs=[full(d), _const_spec(mod.shape),
                  _const_spec(wm.shape), _const_spec(wl.shape), _const_spec(wg2.shape),
                  _const_spec(bg.shape)],
        out_specs=tuple(full(c) for c in widths),
        out_shape=tuple(jax.ShapeDtypeStruct((n, c), F32) for c in widths),
        compiler_params=_cparams("arbitrary"),
        name="proj_sample",
    )(x2d, mod, wm, wl, wg2, bg)


def _sb_scan_matrix():
    k = SB_K_TILE
    s = np.arange(k)[:, None]
    j = np.arange(k)[None, :]
    upper = (j >= s).astype(np.float32)
    top = np.concatenate([upper, upper], axis=1)
    ones = np.ones((16, 2 * k), np.float32)
    return jnp.asarray(np.concatenate([top, ones], axis=0), dtype=BF16)


def _sb_prompt_kernel(bias_ref, qT_ref, k_ref, vT_ref, a_ref, o_ref):
    hg = pl.program_id(1)
    i = pl.program_id(2)
    tq, tk, hd, nh = SB_Q_TILE, SB_K_TILE, SB_HEAD_DIM, SB_HEADS_PER_STEP
    n_diag = tq // tk
    assert n_diag == 2
    scan = a_ref[...]
    head_row = lax.broadcasted_iota(jnp.int32, (2 * hd, tq), 0)
    key_in_blk = lax.broadcasted_iota(jnp.int32, (tk, tq), 0)
    q_in_tile = lax.broadcasted_iota(jnp.int32, (tk, tq), 1)
    diag_valid = [(key_in_blk + (n_diag - 1 - u) * tk) < q_in_tile for u in range(n_diag)]

    bias_row = lax.broadcasted_iota(jnp.int32, (2 * hd, tq), 0)
    ones_cols = jnp.where(lax.broadcasted_iota(jnp.int32, (tk, 2 * hd), 1) < 3, 1.0, 0.0
                          ).astype(BF16)
    qms = []
    for h in range(nh):
        pair = qT_ref[0, (h // 2) * 2 * hd:(h // 2 + 1) * 2 * hd, :].astype(F32)
        own = (head_row < hd) if h % 2 == 0 else (head_row >= hd)
        parts = _split3(jnp.full((2 * hd, tq), bias_ref[nh * hg + h] * LOG2E, F32))
        bias_tile = jnp.zeros((2 * hd, tq), F32)
        for r, part in enumerate(parts):
            bias_tile = jnp.where(bias_row == r, part.astype(F32), bias_tile)
        qms.append(jnp.concatenate([jnp.where(own, pair, 0.0), bias_tile], axis=0).astype(BF16))

    def logits(kb, h):
        lanes = slice((h // 2) * 2 * hd, (h // 2 + 1) * 2 * hd)
        kblk = k_ref[pl.ds(pl.multiple_of(kb * tk, tk), tk), lanes]
        return _dot(jnp.concatenate([kblk, ones_cols], axis=1), qms[h])

    def key_group(first_kb, state, masked):
        carries, accs = list(state[:nh]), list(state[nh:])
        order = [(u, h) for u in range(n_diag) for h in range(nh)]
        zs, mids, rs, ws = {}, {}, {}, {}

        def stage_logits(c):
            u, h = order[c]
            zs[c] = logits(first_kb + (n_diag - 1 - u), h)

        def stage_scan(c):
            u, h = order[c]
            cols = slice(tq // 2, tq) if (masked and u == 0) else slice(0, tq)
            z = zs.pop(c)[:, cols]
            drop = jnp.maximum(z, 0.0) + jnp.log(1.0 + jnp.exp2(_neg_abs(z))) * LOG2E
            valid = None
            if masked:
                valid = diag_valid[u][:, cols]
                drop = jnp.where(valid, drop, 0.0)
            hi, lo = _split2(drop)
            mids[c] = (z, valid, cols)
            rs[c] = _dot(scan, jnp.concatenate([hi, lo], axis=0))

        def stage_values(c):
            u, h = order[c]
            z, valid, cols = mids.pop(c)
            r = rs.pop(c)
            n_cols = cols.stop - cols.start
            w = jnp.exp2((z - r[:tk]).reshape(tk // 8, 8, n_cols) - carries[h][:, cols]
                         ).reshape(tk, n_cols)
            if masked:
                w = jnp.where(valid, w, 0.0)
            total = r[tk:tk + 8]
            if n_cols < tq:
                w = jnp.concatenate([jnp.zeros((tk, tq - n_cols), F32), w], axis=1)
                total = jnp.concatenate([jnp.zeros((8, tq - n_cols), F32), total], axis=1)
            ws.setdefault(h, []).append(w.astype(BF16))
            carries[h] = carries[h] + total
            if u == n_diag - 1:
                v_cat = jnp.concatenate(
                    [vT_ref[first_kb + (n_diag - 1 - uu), h * hd:(h + 1) * hd, :]
                     for uu in range(n_diag)], axis=1)
                accs[h] = accs[h] + _dot(v_cat, jnp.concatenate(ws.pop(h), axis=0))

        n_chains = len(order)
        for s in range(n_chains + 2 * SB_CHAIN_SKEW):
            if s < n_chains:
                stage_logits(s)
            if 0 <= s - SB_CHAIN_SKEW < n_chains:
                stage_scan(s - SB_CHAIN_SKEW)
            if 0 <= s - 2 * SB_CHAIN_SKEW < n_chains:
                stage_values(s - 2 * SB_CHAIN_SKEW)
        return tuple(carries) + tuple(accs)

    state = tuple([jnp.zeros((8, tq), F32)] * nh + [jnp.zeros((hd, tq), F32)] * nh)
    state = key_group(n_diag * i, state, True)
    state = lax.fori_loop(0, i, lambda t, st: key_group(n_diag * (i - 1 - t), st, False), state)

    oT = jnp.concatenate(state[nh:], axis=0)
    o_ref[...] = oT.T.astype(o_ref.dtype)


def _sb_prompt(bias, qT3, k_bf, vT3, batch, seq_len):
    n = k_bf.shape[0]
    tq, tk = SB_Q_TILE, SB_K_TILE
    assert seq_len % tq == 0
    qt = seq_len // tq
    kblocks = seq_len // tk
    width = SB_HEADS_PER_STEP * SB_HEAD_DIM
    scan = _sb_scan_matrix()
    return pl.pallas_call(
        _sb_prompt_kernel,
        grid=(batch, SB_HEADS // SB_HEADS_PER_STEP, qt),
        in_specs=[pl.BlockSpec(memory_space=pltpu.SMEM),
                  pl.BlockSpec((1, width, tq), lambda b, hg, i: (b * qt + i, hg, 0)),
                  pl.BlockSpec((seq_len, width), lambda b, hg, i: (b, hg)),
                  pl.BlockSpec((kblocks, width, tk), lambda b, hg, i: (b, hg, 0)),
                  _const_spec(scan.shape)],
        out_specs=pl.BlockSpec((tq, width), lambda b, hg, i: (b * qt + i, hg)),
        out_shape=jax.ShapeDtypeStruct((n, SB_WIDTH), BF16),
        compiler_params=_cparams("parallel", "parallel", "arbitrary"),
        name="sb_prompt",
    )(bias, qT3, k_bf, vT3, scan)


def _gla_consts():
    t = np.arange(GLA_BLOCK)
    same = (t[:, None] // GLA_CHUNK) == (t[None, :] // GLA_CHUNK)
    tri = (same & (t[None, :] <= t[:, None])).astype(np.float32)
    ones = same.astype(np.float32)
    scan = np.concatenate([np.concatenate([tri] * 3, axis=1),
                           np.concatenate([ones] * 3, axis=1)], axis=0)
    n_chunks = GLA_BLOCK // GLA_CHUNK
    ind = np.zeros((GLA_BLOCK, n_chunks * LANES), np.float32)
    for c in range(n_chunks):
        ind[c * GLA_CHUNK:(c + 1) * GLA_CHUNK, c * LANES:(c + 1) * LANES] = 1.0
    ind = np.concatenate([ind] * 3, axis=0)
    return jnp.asarray(scan, dtype=BF16), jnp.asarray(ind, dtype=BF16)


def _gla_prompt_kernel(gq_ref, gk_ref, gv_ref, la_ref, gr_ref, ng_ref, scan_ref, ind_ref,
                       og_ref, st_ref, s_scr):
    j = pl.program_id(1)
    tb, c_len = GLA_BLOCK, GLA_CHUNK
    n_chunks = tb // c_len
    dk, dv, nh = GLA_HEAD_K, GLA_HEAD_V, GLA_HEADS

    @pl.when(j == 0)
    def _():
        s_scr[...] = jnp.zeros_like(s_scr)

    la = la_ref[...]
    parts = jnp.concatenate(_split3(la), axis=0)
    bb = _dot(scan_ref[...], parts)
    b = bb[:tb]
    b_last = bb[tb:]
    laT_parts = jnp.concatenate(_split3(la.T), axis=1)
    b_last_col = _dot(laT_parts, ind_ref[...])

    q = gq_ref[...] * (dk ** -0.5)
    k = gk_ref[...]
    v = gv_ref[...]
    q_dec = q * jnp.exp(b)
    k_inv = (k * jnp.exp(-b)).astype(BF16)
    k_endT = (k * jnp.exp(b_last - b)).T

    lane_head = _div_pow2(lax.broadcasted_iota(jnp.int32, (tb, nh * dk), 1), dk)
    ti = lax.broadcasted_iota(jnp.int32, (tb, tb), 0)
    tj = lax.broadcasted_iota(jnp.int32, (tb, tb), 1)
    tj_chunk = _div_pow2(tj, c_len)
    causal = (_div_pow2(ti, c_len) == tj_chunk) & (tj <= ti)

    qm = [jnp.where(lane_head == h, q_dec, 0.0).astype(BF16) for h in range(nh)]
    scores = [_dot_nt(qm[h], k_inv) for h in range(nh)]
    us = []
    for c in range(n_chunks):
        k_endT_c = jnp.where(tj_chunk == c, k_endT, 0.0).astype(BF16)
        us.append(_dot(k_endT_c, v))
    o_intra = [_dot(jnp.where(causal, scores[h], 0.0).astype(BF16), v[:, h * dv:(h + 1) * dv])
               for h in range(nh)]

    states = [s_scr[...]]
    for c in range(n_chunks):
        u_diag = jnp.concatenate(
            [us[c][h * dk:(h + 1) * dk, h * dv:(h + 1) * dv] for h in range(nh)], axis=0)
        states.append(jnp.exp(b_last_col[:, c * LANES:(c + 1) * LANES]) * states[c] + u_diag)
    o_inter = []
    for c in range(n_chunks):
        rows = slice(c * c_len, (c + 1) * c_len)
        q_stack = jnp.concatenate([qm[h][rows] for h in range(nh)], axis=0)
        o_inter.append(_dot(q_stack, states[c].astype(BF16)))

    norm_g = ng_ref[...]
    for c in range(n_chunks):
        rows = slice(c * c_len, (c + 1) * c_len)
        for h in range(nh):
            o = o_intra[h][rows] + o_inter[c][h * c_len:(h + 1) * c_len]
            o = o * lax.rsqrt(jnp.mean(o * o, axis=-1, keepdims=True) + GN_EPS) * norm_g
            r = gr_ref[rows, h * dv:(h + 1) * dv]
            og_ref[rows, h * dv:(h + 1) * dv] = (o * (r * _sigmoid(r))).astype(og_ref.dtype)
    s_all = states[n_chunks]
    s_scr[...] = s_all

    @pl.when(j == pl.num_programs(1) - 1)
    def _():
        st_ref[0] = s_all


def _gla_prompt(gq, gk, gv, la, gr, norm_g, batch, seq_len):
    n = gq.shape[0]
    tb = GLA_BLOCK
    assert seq_len % tb == 0
    nb = seq_len // tb
    scan, ind = _gla_consts()
    row = lambda cols: pl.BlockSpec((tb, cols), lambda b, j: (b * nb + j, 0))
    rows_s = GLA_HEADS * GLA_HEAD_K
    return pl.pallas_call(
        _gla_prompt_kernel,
        grid=(batch, nb),
        in_specs=[row(GLA_K_WIDTH), row(GLA_K_WIDTH), row(GLA_V_WIDTH), row(GLA_K_WIDTH),
                  row(GLA_V_WIDTH), _const_spec((1, GLA_HEAD_V)),
                  _const_spec(scan.shape), _const_spec(ind.shape)],
        out_specs=(row(GLA_V_WIDTH),
                   pl.BlockSpec((1, rows_s, GLA_HEAD_V), lambda b, j: (b, 0, 0))),
        out_shape=(jax.ShapeDtypeStruct((n, GLA_V_WIDTH), BF16),
                   jax.ShapeDtypeStruct((batch, rows_s, GLA_HEAD_V), F32)),
        scratch_shapes=[pltpu.VMEM((rows_s, GLA_HEAD_V), F32)],
        compiler_params=_cparams("parallel", "arbitrary"),
        name="gla_prompt",
    )(gq, gk, gv, la, gr, norm_g.reshape(1, GLA_HEAD_V), scan, ind)


POST_SEGMENTS = 8


def _post_math(alpha, x_ref, osb_ref, og_ref, mod_ref, wg_ref, wua_ref, wub_ref, wo_ref,
               ln1g_ref, ln1b_ref, wf1_ref, wf2_ref, ln2g_ref, ln2b_ref, y_ref, segment):
    x = x_ref[...]
    sh1, sc1, gt1, sh2, sc2, gt2 = [mod_ref[0, m] for m in range(6)]
    d = x.shape[-1]
    assert sum(FF_CHUNKS) == wf1_ref.shape[1] and POST_SEGMENTS == 3 + len(FF_CHUNKS)

    def seg0():
        h = (_ln_stats(x) * (1.0 + sc1) + sh1).astype(BF16)
        return h, _sigmoid(_dot(h, wg_ref[:, :d]))

    h, gate_a = segment(0, seg0)
    part_a, gate_b = segment(1, lambda: (gate_a * _dot(osb_ref[...].astype(BF16), wua_ref[...]),
                                         _sigmoid(_dot(h, wg_ref[:, d:]))))

    def seg2():
        merged = (part_a + gate_b * _dot(og_ref[...].astype(BF16), wub_ref[...])).astype(BF16)
        x1 = _ln_stats(alpha * x + gt1 * _dot(merged, wo_ref[...])) * ln1g_ref[...] + ln1b_ref[...]
        return x1, (_ln_stats(x1) * (1.0 + sc2) + sh2).astype(BF16)

    x1, h2 = segment(2, seg2)
    f = jnp.zeros_like(x)
    lo = 0
    for c, width in enumerate(FF_CHUNKS):
        cols = slice(lo, lo + width)
        lo += width

        def ff(f=f, cols=cols, last=(c == len(FF_CHUNKS) - 1)):
            u = jnp.maximum(_dot(h2, wf1_ref[:, cols]), 0.0)
            f_new = f + _dot((u * u).astype(BF16), wf2_ref[cols, :])
            if last:
                y_ref[...] = _ln_stats(alpha * x1 + gt2 * f_new) * ln2g_ref[...] + ln2b_ref[...]
            return f_new

        f = segment(3 + c, ff)


def _post_kernel(alpha, *refs):
    _post_math(alpha, *refs, segment=lambda j, fn: fn())


def _post(x2d, o_sb, o_g, mod, seq_len, alpha, wg, wua, wub, wo, ln1g, ln1b, wf1, wf2, ln2g, ln2b):
    n, d = x2d.shape
    per_token = mod.shape[2] != 1
    tm = n if per_token else min(TOKEN_TILE, seq_len)
    tiles_per_seq = 1 if per_token else seq_len // tm
    row = lambda cols: pl.BlockSpec((tm, cols), lambda i: (i, 0))
    single = lambda a: pl.BlockSpec(a.shape, lambda i: (0,) * a.ndim, pipeline_mode=pl.Buffered(1))
    vec = lambda a: a.reshape(1, d)
    mod_spec = (_const_spec(mod.shape) if per_token else
                pl.BlockSpec((1, 6, 1, d), lambda i: (i // tiles_per_seq, 0, 0, 0)))
    return pl.pallas_call(
        functools.partial(_post_kernel, alpha),
        grid=(n // tm,),
        in_specs=[row(d), row(SB_WIDTH), row(GLA_V_WIDTH), mod_spec,
                  single(wg), single(wua), single(wub), single(wo),
                  _const_spec((1, d)), _const_spec((1, d)),
                  single(wf1), single(wf2),
                  _const_spec((1, d)), _const_spec((1, d))],
        out_specs=row(d),
        out_shape=jax.ShapeDtypeStruct((n, d), F32),
        compiler_params=_cparams("parallel"),
        name="post",
    )(x2d, o_sb, o_g, mod, wg, wua, wub, wo, vec(ln1g), vec(ln1b), wf1, wf2, vec(ln2g), vec(ln2b))


def _lane_scan_matrix():
    j = np.arange(LANES)[:, None]
    s = np.arange(LANES)[None, :]
    later = (j > s).astype(np.float32)
    half = np.concatenate([later, np.ones((LANES, LANES), np.float32)], axis=1)
    return jnp.asarray(np.concatenate([half, half], axis=0), dtype=BF16)


def _paged_logits(k_pages, qb_scr, bias_ref, lb_scr, hl_scr):
    p, nh, hd = PAGED_PAGES, SB_HEADS, SB_HEAD_DIM
    sublane = lax.broadcasted_iota(jnp.int32, (nh, LANES), 0)
    zs = [jnp.zeros((nh, LANES), F32)] * p
    bias = jnp.zeros((nh, LANES), F32)
    for h in range(nh):
        hrows = slice(h * hd, (h + 1) * hd)
        qh = qb_scr[hrows, :]
        bias = jnp.where(sublane == h, bias_ref[h], bias)
        for s in range(p):
            tot = jnp.sum(k_pages[s, hrows, :] * qh, axis=0, keepdims=True)
            zs[s] = jnp.where(sublane == h, tot, zs[s])
    z = jnp.concatenate([zs[s] + bias for s in range(p)], axis=0)
    soft = jnp.log1p(jnp.exp(-jnp.abs(z)))
    log_beta = jnp.minimum(z, 0.0) - soft
    hi, lo = _split2(log_beta - z)
    lb_scr[...] = log_beta
    hl_scr[...] = jnp.concatenate([hi, lo], axis=1)


def _paged_accumulate(v_pages, r1, first, lb_scr, carry_scr, acc_scr):
    p, nh, hd = PAGED_PAGES, SB_HEADS, SB_HEAD_DIM
    carry = jnp.where(first, 0.0, carry_scr[...])
    row_tot = r1[:, LANES:]
    prefix, run = [], carry
    for s in range(p):
        prefix.append(run)
        run = run + row_tot[s * nh:(s + 1) * nh]
    carry_scr[...] = run
    w = jnp.exp(lb_scr[...] + r1[:, :LANES] + jnp.concatenate(prefix, axis=0))
    for h in range(nh):
        hrows = slice(h * hd, (h + 1) * hd)
        acc = jnp.where(first, 0.0, acc_scr[hrows, :])
        for s in range(p):
            acc = acc + v_pages[s, hrows, :] * w[s * nh + h:s * nh + h + 1, :]
        acc_scr[hrows, :] = acc


def _post_paged_kernel(alpha, n_pages, pt_ref, bias_ref, *refs):
    post_refs = refs[:14]
    q_ref, lscan_ref, kc_hbm, vc_hbm = refs[14:18]
    y_ref, os_ref = refs[18:20]
    kbuf, vbuf, ksem, vsem, carry_scr, acc_scr, qb_scr, lb_scr, hl_scr = refs[20:]
    p, subs, slots = PAGED_PAGES, POST_SEGMENTS, PAGE_SLOTS
    groups = n_pages // p
    i = pl.program_id(0)
    n_steps = pl.num_programs(0)

    def slot_of(t):
        return lax.rem(t + slots, slots)

    def copies(hbm, buf, sem, t):
        b, g, slot = lax.div(t, groups), lax.rem(t, groups), slot_of(t)
        return [pltpu.make_async_copy(hbm.at[pt_ref[b, n_pages - 1 - (g * p + s)]],
                                      buf.at[slot, s], sem.at[slot]) for s in range(p)]

    def start(cs):
        for c in cs:
            c.start()

    def wait(cs):
        for c in cs:
            c.wait()

    def finish_sequence(b):
        acc_t = acc_scr[...].T
        os_ref[pl.ds(b, 1), :] = jnp.sum(acc_t, axis=0, keepdims=True)

    scanned = {}

    def before(j):
        t = subs * i + j
        if j == 0:
            @pl.when(i == 0)
            def _():
                hl_scr[...] = jnp.zeros_like(hl_scr)
                lb_scr[...] = jnp.zeros_like(lb_scr)
                carry_scr[...] = jnp.zeros_like(carry_scr)
                acc_scr[...] = jnp.zeros_like(acc_scr)
                vbuf[slots - 1] = jnp.zeros(vbuf.shape[1:], vbuf.dtype)
                start(copies(kc_hbm, kbuf, ksem, t))
                start(copies(kc_hbm, kbuf, ksem, t + 1))
                start(copies(vc_hbm, vbuf, vsem, t))

            @pl.when(i > 0)
            def _():
                wait(copies(vc_hbm, vbuf, vsem, t - 1))
        else:
            wait(copies(vc_hbm, vbuf, vsem, t - 1))
        wait(copies(kc_hbm, kbuf, ksem, t))
        for ahead, hbm, buf, sem in ((2, kc_hbm, kbuf, ksem), (1, vc_hbm, vbuf, vsem)):
            if j + ahead < subs:
                start(copies(hbm, buf, sem, t + ahead))
            else:
                @pl.when(i + 1 < n_steps)
                def _(ahead=ahead, hbm=hbm, buf=buf, sem=sem):
                    start(copies(hbm, buf, sem, t + ahead))
        scanned[j] = _dot(hl_scr[...], lscan_ref[...])

    def after(j):
        t = subs * i + j
        first = lax.rem(t - 1 + groups, groups) == 0
        _paged_accumulate(vbuf.at[slot_of(t - 1)], scanned[j], first, lb_scr, carry_scr, acc_scr)
        if j == 0:
            @pl.when(jnp.logical_and(lax.rem(t, groups) == 0, t > 0))
            def _():
                finish_sequence(lax.div(t, groups) - 1)

            @pl.when(lax.rem(t, groups) == 0)
            def _():
                q_row = q_ref[pl.ds(lax.div(t, groups), 1), :]
                qb_scr[...] = jnp.broadcast_to(q_row, (LANES, q_row.shape[1])).T
        _paged_logits(kbuf.at[slot_of(t)], qb_scr, bias_ref, lb_scr, hl_scr)

    def segment(j, fn):
        before(j)
        out = fn()
        after(j)
        return out

    _post_math(alpha, *post_refs, y_ref, segment=segment)

    @pl.when(i == n_steps - 1)
    def _():
        t_last = subs * n_steps - 1
        wait(copies(vc_hbm, vbuf, vsem, t_last))
        r1 = _dot(hl_scr[...], lscan_ref[...])
        _paged_accumulate(vbuf.at[slot_of(t_last)], r1, False, lb_scr, carry_scr, acc_scr)
        finish_sequence(lax.div(t_last, groups))


def _post_with_paged_attention(x2d, o_sb, o_g, mod, seq_len, alpha, post_w,
                               page_table, bias, q_s, cache_kT, cache_vT):
    wg, wua, wub, wo, ln1g, ln1b, wf1, wf2, ln2g, ln2b = post_w
    n, d = x2d.shape
    nseq, n_pages = page_table.shape
    p, width = PAGED_PAGES, SB_WIDTH
    tm = min(TOKEN_TILE, seq_len)
    tiles_per_seq = seq_len // tm
    steps = n // tm
    assert cache_kT.shape[1:] == (width, LANES) and n_pages % p == 0
    assert (n_pages // p) % POST_SEGMENTS == 0 and steps * POST_SEGMENTS * p == nseq * n_pages
    lscan = _lane_scan_matrix()

    row = lambda cols: pl.BlockSpec((tm, cols), lambda i, pt: (i, 0))
    const = lambda shape: pl.BlockSpec(shape, lambda i, pt: (0,) * len(shape))
    single = lambda a: pl.BlockSpec(a.shape, lambda i, pt: (0,) * a.ndim,
                                    pipeline_mode=pl.Buffered(1))
    vec = lambda a: a.reshape(1, d)
    hbm = pl.BlockSpec(memory_space=pl.ANY)
    grid_spec = pltpu.PrefetchScalarGridSpec(
        num_scalar_prefetch=1,
        grid=(steps,),
        in_specs=[pl.BlockSpec(memory_space=pltpu.SMEM),
                  row(d), row(SB_WIDTH), row(GLA_V_WIDTH),
                  pl.BlockSpec((1, 6, 1, d), lambda i, pt: (i // tiles_per_seq, 0, 0, 0)),
                  single(wg), single(wua), single(wub), single(wo),
                  const((1, d)), const((1, d)), single(wf1), single(wf2),
                  const((1, d)), const((1, d)),
                  const(q_s.shape), const(lscan.shape), hbm, hbm],
        out_specs=(row(d), const((nseq, width))),
        scratch_shapes=[pltpu.VMEM((PAGE_SLOTS, p, width, LANES), F32),
                        pltpu.VMEM((PAGE_SLOTS, p, width, LANES), F32),
                        pltpu.SemaphoreType.DMA((PAGE_SLOTS,)),
                        pltpu.SemaphoreType.DMA((PAGE_SLOTS,)),
                        pltpu.VMEM((SB_HEADS, LANES), F32),
                        pltpu.VMEM((width, LANES), F32),
                        pltpu.VMEM((width, LANES), F32),
                        pltpu.VMEM((p * SB_HEADS, LANES), F32),
                        pltpu.VMEM((p * SB_HEADS, 2 * LANES), BF16)],
    )
    return pl.pallas_call(
        functools.partial(_post_paged_kernel, alpha, n_pages),
        grid_spec=grid_spec,
        out_shape=(jax.ShapeDtypeStruct((n, d), F32), jax.ShapeDtypeStruct((nseq, width), F32)),
        compiler_params=_cparams("arbitrary"),
        name="post_paged",
    )(page_table, bias, x2d, o_sb, o_g, mod, wg, wua, wub, wo, vec(ln1g), vec(ln1b), wf1, wf2,
      vec(ln2g), vec(ln2b), q_s, lscan, cache_kT, cache_vT)


def _gla_sample_kernel(gq_ref, gk_ref, la_ref, gv_ref, gr_ref, ng_ref, s0_ref, og_ref, st_ref):
    nh, dk, dv = GLA_HEADS, GLA_HEAD_K, GLA_HEAD_V
    la = la_ref[0]
    q_dec = gq_ref[0] * (dk ** -0.5) * jnp.exp(la)
    k_inv = gk_ref[0] * jnp.exp(-la)
    k_end = gk_ref[0]
    s0 = s0_ref[0]
    v8 = gv_ref[0]
    qk = q_dec * k_inv
    qs = q_dec * s0
    outs = []
    for h in range(nh):
        rows = slice(h * dk, (h + 1) * dk)
        att = jnp.sum(qk[rows], axis=0, keepdims=True)
        outs.append(att * v8[h:h + 1] + jnp.sum(qs[rows], axis=0, keepdims=True))
    o = jnp.concatenate(outs + [jnp.zeros((8 - nh, dv), F32)], axis=0)
    o = o * lax.rsqrt(jnp.mean(o * o, axis=-1, keepdims=True) + GN_EPS) * ng_ref[...]
    r = gr_ref[0]
    og_ref[0] = o * (r * _sigmoid(r))
    v_rows = jnp.concatenate(
        [jnp.broadcast_to(v8[h:h + 1], (dk, dv)) for h in range(nh)], axis=0)
    st_ref[0] = jnp.exp(la) * s0 + k_end * v_rows


def _gla_sample(gq, gk, gv, la, gr, norm_g, s0):
    n = gq.shape[0]
    nh, dk, dv = GLA_HEADS, GLA_HEAD_K, GLA_HEAD_V
    kw = nh * dk
    pad_heads = lambda a: jnp.pad(a.reshape(n, nh, dv), ((0, 0), (0, 8 - nh), (0, 0)))
    colspec = pl.BlockSpec((1, kw, 1), lambda b: (b, 0, 0))
    headspec = pl.BlockSpec((1, 8, dv), lambda b: (b, 0, 0))
    statespec = pl.BlockSpec((1, kw, dv), lambda b: (b, 0, 0))
    og, st = pl.pallas_call(
        _gla_sample_kernel,
        grid=(n,),
        in_specs=[colspec, colspec, colspec, headspec, headspec, _const_spec((1, dv)), statespec],
        out_specs=(headspec, statespec),
        out_shape=(jax.ShapeDtypeStruct((n, 8, dv), F32),
                   jax.ShapeDtypeStruct((n, kw, dv), F32)),
        compiler_params=_cparams("parallel"),
        name="gla_sample",
    )(gq.reshape(n, kw, 1), gk.reshape(n, kw, 1), la.reshape(n, kw, 1),
      pad_heads(gv), pad_heads(gr), norm_g.reshape(1, dv), s0)
    return og[:, :nh].reshape(n, nh * dv), st


def kernel(x_prompt, x_sample, cache_k, cache_v, state_gla, page_table, c_prompt, c_sample,
           w_ada, b_ada, w_in, sb_bias, w_gla_g2, b_gla_g, gla_norm_g, w_up_a, w_up_b, w_o,
           ln1_g, ln1_b, w_ff1, w_ff2, ln2_g, ln2_b):
    depth = w_ada.shape[0]
    pb, seq_len, d = x_prompt.shape
    sb_n = x_sample.shape[0]
    assert x_sample.shape[1] == 1
    alpha = (2.0 * depth) ** 0.25
    n_phys, page_size = cache_k.shape[1], cache_k.shape[2]

    yp = x_prompt.reshape(pb * seq_len, d)
    ys = x_sample.reshape(sb_n, d)
    c_all = jnp.concatenate([c_prompt, c_sample], axis=0)
    main_w = 3 * SB_WIDTH + 2 * GLA_K_WIDTH + 2 * GLA_V_WIDTH
    gate_off = main_w + GLA_GATE_RANK

    outs = [[] for _ in range(6)]
    for l in range(depth):
        ada = _ada(c_all, w_ada[l].astype(BF16), b_ada[l])
        mod_p = ada[:pb].reshape(pb, 6, 1, d)
        mod_s = ada[pb:].reshape(sb_n, 6, d).transpose(1, 0, 2).reshape(1, 6, sb_n, d)

        w_in_l = w_in[l]
        wm = w_in_l[:, :main_w].astype(BF16)
        wl = jnp.pad(w_in_l[:, main_w:gate_off], ((0, 0), (0, LANES - GLA_GATE_RANK))).astype(BF16)
        wg = w_in_l[:, gate_off:].astype(BF16)
        wg2 = jnp.pad(w_gla_g2[l], ((0, LANES - GLA_GATE_RANK), (0, 0))).astype(BF16)
        bg = b_gla_g[l].reshape(1, GLA_K_WIDTH)
        post_w = (wg, w_up_a[l].astype(BF16), w_up_b[l].astype(BF16), w_o[l].astype(BF16),
                  ln1_g[l], ln1_b[l], w_ff1[l].astype(BF16), w_ff2[l].astype(BF16),
                  ln2_g[l], ln2_b[l])

        qT3, k_p, v_p, k_bf, vT3, gq, gk, gv, gr, la = _proj_prompt(yp, mod_p, seq_len, wm, wl, wg2, bg)
        q_s, k_s, v_s, gq_s, gk_s, gv_s, gr_s, la_s = _proj_sample(ys, mod_s, wm, wl, wg2, bg)
        o_sb = _sb_prompt(sb_bias[l], qT3, k_bf, vT3, pb, seq_len)
        o_g, st_p = _gla_prompt(gq, gk, gv, la, gr, gla_norm_g[l], pb, seq_len)

        page_major = lambda c: jnp.transpose(c, (0, 2, 3, 1)).reshape(n_phys, SB_WIDTH, page_size)
        yp, o_sb_s = _post_with_paged_attention(
            yp, o_sb, o_g, mod_p, seq_len, alpha, post_w, page_table, sb_bias[l], q_s,
            page_major(cache_k[l]), page_major(cache_v[l]))

        s0 = state_gla[l].reshape(sb_n, GLA_HEADS * GLA_HEAD_K, GLA_HEAD_V)
        o_g_s, st_s = _gla_sample(gq_s, gk_s, gv_s, la_s, gr_s, gla_norm_g[l], s0)
        ys = _post(ys, o_sb_s, o_g_s, mod_s, 1, alpha, *post_w)

        rows_major = lambda t: jnp.transpose(
            t.reshape(pb, SB_HEADS, SB_HEAD_DIM, seq_len), (0, 3, 1, 2))
        outs[0].append(rows_major(k_p))
        outs[1].append(rows_major(v_p))
        outs[2].append(st_p.reshape(pb, GLA_HEADS, GLA_HEAD_K, GLA_HEAD_V))
        outs[3].append(k_s.reshape(sb_n, 1, SB_HEADS, SB_HEAD_DIM))
        outs[4].append(v_s.reshape(sb_n, 1, SB_HEADS, SB_HEAD_DIM))
        outs[5].append(st_s.reshape(sb_n, GLA_HEADS, GLA_HEAD_K, GLA_HEAD_V))

    stacked = [jnp.stack(o) for o in outs]
    return (yp.reshape(pb, seq_len, d), ys.reshape(sb_n, 1, d),
            stacked[0], stacked[1], stacked[2].astype(state_gla.dtype),
            stacked[3], stacked[4], stacked[5].astype(state_gla.dtype))
```

```python
import functools

import numpy as np
import jax
import jax.numpy as jnp
from jax import lax
from jax.experimental import pallas as pl
from jax.experimental.pallas import tpu as pltpu

F32 = jnp.float32
BF16 = jnp.bfloat16

SB_HEADS = 8
SB_HEAD_DIM = 64
SB_WIDTH = SB_HEADS * SB_HEAD_DIM
GLA_HEADS = 4
GLA_HEAD_K = 64
GLA_HEAD_V = 128
GLA_K_WIDTH = GLA_HEADS * GLA_HEAD_K
GLA_V_WIDTH = GLA_HEADS * GLA_HEAD_V
GLA_GATE_RANK = 16
GLA_TAU = 16.0
GLA_CHUNK = 64
LN_EPS = 1e-5
GN_EPS = 1e-6

LANES = 128
VMEM_LIMIT_BYTES = 56 * 1024 * 1024

TOKEN_TILE = 512
PROJ_TOKEN_TILE = 1024
SB_Q_TILE = 256
SB_K_TILE = 128
SB_CHAIN_SKEW = 12
SB_HEADS_PER_STEP = 8
GLA_BLOCK = 256
PAGED_PAGES = 8
FF_CHUNKS = (768, 768, 768, 768, 1024)
PAGE_SLOTS = 3


def _cparams(*sem):
    return pltpu.CompilerParams(dimension_semantics=sem, vmem_limit_bytes=VMEM_LIMIT_BYTES)


def _dot(a, b):
    return jnp.dot(a, b, preferred_element_type=F32)


def _dot_nt(a, b):
    return lax.dot_general(a, b, (((1,), (1,)), ((), ())), preferred_element_type=F32)


def _ln_stats(x):
    mu = jnp.mean(x, axis=-1, keepdims=True)
    xc = x - mu
    var = jnp.mean(xc * xc, axis=-1, keepdims=True)
    return xc * lax.rsqrt(var + LN_EPS)


def _log_sigmoid(x):
    return jnp.minimum(x, 0.0) - jnp.log1p(jnp.exp(-jnp.abs(x)))


def _sigmoid(x):
    return 1.0 / (1.0 + jnp.exp(-x))


LOG2E = 1.4426950408889634


def _neg_abs(x):
    bits = lax.bitcast_convert_type(x, jnp.int32) | jnp.int32(-2 ** 31)
    return lax.bitcast_convert_type(bits, F32)


def _div_pow2(x, n):
    assert n & (n - 1) == 0
    return jnp.right_shift(x, n.bit_length() - 1)


def _split2(x):
    hi = x.astype(BF16)
    lo = (x - hi.astype(F32)).astype(BF16)
    return hi, lo


def _split3(x):
    hi = x.astype(BF16)
    r = x - hi.astype(F32)
    mid = r.astype(BF16)
    lo = (r - mid.astype(F32)).astype(BF16)
    return hi, mid, lo


def _ada_kernel(c_ref, w_ref, b_ref, o_ref):
    c = c_ref[...]
    a = c * _sigmoid(c)
    o_ref[...] = _dot(a.astype(BF16), w_ref[...]) + b_ref[...]


def _ada(c_all, w_ada_b, b_ada):
    n, d = c_all.shape
    nout = w_ada_b.shape[1]
    blk = d
    return pl.pallas_call(
        _ada_kernel,
        grid=(nout // blk,),
        in_specs=[pl.BlockSpec((n, d), lambda j: (0, 0)),
                  pl.BlockSpec((d, blk), lambda j: (0, j)),
                  pl.BlockSpec((1, blk), lambda j: (0, j))],
        out_specs=pl.BlockSpec((n, blk), lambda j: (0, j)),
        out_shape=jax.ShapeDtypeStruct((n, nout), F32),
        compiler_params=_cparams("parallel"),
        name="ada",
    )(c_all, w_ada_b, b_ada.reshape(1, nout))


def _proj_math(x, sh1, sc1, wm_ref, wl_ref, wg2_ref, bg_ref):
    h = (_ln_stats(x) * (1.0 + sc1) + sh1).astype(BF16)
    y = _dot(h, wm_ref[...])
    g_low = _dot(h, wl_ref[...])
    gl = _dot(g_low.astype(BF16), wg2_ref[...]) + bg_ref[...]
    log_a = _log_sigmoid(gl) * (1.0 / GLA_TAU)
    return y, log_a


def _proj_prompt_kernel(x_ref, mod_ref, wm_ref, wl_ref, wg2_ref, bg_ref,
                        qT_ref, kT_ref, vT_ref, kb_ref, vTb_ref, gq_ref, gk_ref, gv_ref, gr_ref, la_ref):
    y, log_a = _proj_math(x_ref[...], mod_ref[0, 0], mod_ref[0, 1], wm_ref, wl_ref, wg2_ref, bg_ref)
    w = SB_WIDTH
    q = y[:, 0:w] * (SB_HEAD_DIM ** -0.5 * LOG2E)
    k = y[:, w:2 * w]
    v = y[:, 2 * w:3 * w]
    kb_ref[...] = k.astype(BF16)
    qT = q.T.astype(BF16)
    vT = v.T
    kT_ref[0] = k.T
    vT_ref[0] = vT
    vT = vT.astype(BF16)
    for c in range(qT_ref.shape[0]):
        qT_ref[c] = qT[:, c * SB_Q_TILE:(c + 1) * SB_Q_TILE]
    for c in range(vTb_ref.shape[0]):
        vTb_ref[c] = vT[:, c * SB_K_TILE:(c + 1) * SB_K_TILE]
    o = 3 * w
    gq_ref[...] = y[:, o:o + GLA_K_WIDTH]
    gk_ref[...] = y[:, o + GLA_K_WIDTH:o + 2 * GLA_K_WIDTH]
    o += 2 * GLA_K_WIDTH
    gv_ref[...] = y[:, o:o + GLA_V_WIDTH].astype(BF16)
    gr_ref[...] = y[:, o + GLA_V_WIDTH:o + 2 * GLA_V_WIDTH]
    la_ref[...] = log_a


def _proj_sample_kernel(x_ref, mod_ref, wm_ref, wl_ref, wg2_ref, bg_ref,
                        q_ref, k_ref, v_ref, gq_ref, gk_ref, gv_ref, gr_ref, la_ref):
    y, log_a = _proj_math(x_ref[...], mod_ref[0, 0], mod_ref[0, 1], wm_ref, wl_ref, wg2_ref, bg_ref)
    w = SB_WIDTH
    q_ref[...] = y[:, 0:w] * (SB_HEAD_DIM ** -0.5)
    k_ref[...] = y[:, w:2 * w]
    v_ref[...] = y[:, 2 * w:3 * w]
    o = 3 * w
    gq_ref[...] = y[:, o:o + GLA_K_WIDTH]
    gk_ref[...] = y[:, o + GLA_K_WIDTH:o + 2 * GLA_K_WIDTH]
    o += 2 * GLA_K_WIDTH
    gv_ref[...] = y[:, o:o + GLA_V_WIDTH]
    gr_ref[...] = y[:, o + GLA_V_WIDTH:o + 2 * GLA_V_WIDTH]
    la_ref[...] = log_a


def _const_spec(shape):
    nd = len(shape)
    return pl.BlockSpec(shape, lambda *a: (0,) * nd)


def _proj_prompt(x2d, mod, seq_len, wm, wl, wg2, bg):
    n, d = x2d.shape
    tm = min(PROJ_TOKEN_TILE, seq_len)
    assert seq_len % tm == 0 and tm % SB_Q_TILE == 0
    tiles_per_seq = seq_len // tm
    row = lambda cols: pl.BlockSpec((tm, cols), lambda i: (i, 0))
    dims_major = pl.BlockSpec((1, SB_WIDTH, tm),
                              lambda i: (i // tiles_per_seq, 0, i % tiles_per_seq))
    out_shape = (
        jax.ShapeDtypeStruct((n // SB_Q_TILE, SB_WIDTH, SB_Q_TILE), BF16),
        jax.ShapeDtypeStruct((n // seq_len, SB_WIDTH, seq_len), F32),
        jax.ShapeDtypeStruct((n // seq_len, SB_WIDTH, seq_len), F32),
        jax.ShapeDtypeStruct((n, SB_WIDTH), BF16),
        jax.ShapeDtypeStruct((n // SB_K_TILE, SB_WIDTH, SB_K_TILE), BF16),
        jax.ShapeDtypeStruct((n, GLA_K_WIDTH), F32),
        jax.ShapeDtypeStruct((n, GLA_K_WIDTH), F32),
        jax.ShapeDtypeStruct((n, GLA_V_WIDTH), BF16),
        jax.ShapeDtypeStruct((n, GLA_V_WIDTH), F32),
        jax.ShapeDtypeStruct((n, GLA_K_WIDTH), F32),
    )
    out_specs = (
        pl.BlockSpec((tm // SB_Q_TILE, SB_WIDTH, SB_Q_TILE), lambda i: (i, 0, 0)),
        dims_major, dims_major, row(SB_WIDTH),
        pl.BlockSpec((tm // SB_K_TILE, SB_WIDTH, SB_K_TILE), lambda i: (i, 0, 0)),
        row(GLA_K_WIDTH), row(GLA_K_WIDTH), row(GLA_V_WIDTH), row(GLA_V_WIDTH), row(GLA_K_WIDTH),
    )
    return pl.pallas_call(
        _proj_prompt_kernel,
        grid=(n // tm,),
        in_specs=[row(d),
                  pl.BlockSpec((1, 6, 1, d), lambda i: (i // tiles_per_seq, 0, 0, 0)),
                  _const_spec(wm.shape), _const_spec(wl.shape), _const_spec(wg2.shape),
                  _const_spec(bg.shape)],
        out_specs=out_specs,
        out_shape=out_shape,
        compiler_params=_cparams("parallel"),
        name="proj_prompt",
    )(x2d, mod, wm, wl, wg2, bg)


def _proj_sample(x2d, mod, wm, wl, wg2, bg):
    n, d = x2d.shape
    full = lambda cols: pl.BlockSpec((n, cols), lambda i: (0, 0))
    widths = (SB_WIDTH, SB_WIDTH, SB_WIDTH, GLA_K_WIDTH, GLA_K_WIDTH, GLA_V_WIDTH, GLA_V_WIDTH,
              GLA_K_WIDTH)
    return pl.pallas_call(
        _proj_sample_kernel,
        grid=(1,),
        in_specs=[full(d), _const_spec(mod.shape),
                  _const_spec(wm.shape), _const_spec(wl.shape), _const_spec(wg2.shape),
                  _const_spec(bg.shape)],
        out_specs=tuple(full(c) for c in widths),
        out_shape=tuple(jax.ShapeDtypeStruct((n, c), F32) for c in widths),
        compiler_params=_cparams("arbitrary"),
        name="proj_sample",
    )(x2d, mod, wm, wl, wg2, bg)


def _sb_scan_matrix():
    k = SB_K_TILE
    s = np.arange(k)[:, None]
    j = np.arange(k)[None, :]
    upper = (j >= s).astype(np.float32)
    top = np.concatenate([upper, upper], axis=1)
    ones = np.ones((16, 2 * k), np.float32)
    return jnp.asarray(np.concatenate([top, ones], axis=0), dtype=BF16)


def _sb_prompt_kernel(bias_ref, qT_ref, k_ref, vT_ref, a_ref, o_ref):
    hg = pl.program_id(1)
    i = pl.program_id(2)
    tq, tk, hd, nh = SB_Q_TILE, SB_K_TILE, SB_HEAD_DIM, SB_HEADS_PER_STEP
    n_diag = tq // tk
    assert n_diag == 2
    scan = a_ref[...]
    head_row = lax.broadcasted_iota(jnp.int32, (2 * hd, tq), 0)
    key_in_blk = lax.broadcasted_iota(jnp.int32, (tk, tq), 0)
    q_in_tile = lax.broadcasted_iota(jnp.int32, (tk, tq), 1)
    diag_valid = [(key_in_blk + (n_diag - 1 - u) * tk) < q_in_tile for u in range(n_diag)]

    bias_row = lax.broadcasted_iota(jnp.int32, (2 * hd, tq), 0)
    ones_cols = jnp.where(lax.broadcasted_iota(jnp.int32, (tk, 2 * hd), 1) < 3, 1.0, 0.0
                          ).astype(BF16)
    qms = []
    for h in range(nh):
        pair = qT_ref[0, (h // 2) * 2 * hd:(h // 2 + 1) * 2 * hd, :].astype(F32)
        own = (head_row < hd) if h % 2 == 0 else (head_row >= hd)
        parts = _split3(jnp.full((2 * hd, tq), bias_ref[nh * hg + h] * LOG2E, F32))
        bias_tile = jnp.zeros((2 * hd, tq), F32)
        for r, part in enumerate(parts):
            bias_tile = jnp.where(bias_row == r, part.astype(F32), bias_tile)
        qms.append(jnp.concatenate([jnp.where(own, pair, 0.0), bias_tile], axis=0).astype(BF16))

    def logits(kb, h):
        lanes = slice((h // 2) * 2 * hd, (h // 2 + 1) * 2 * hd)
        kblk = k_ref[pl.ds(pl.multiple_of(kb * tk, tk), tk), lanes]
        return _dot(jnp.concatenate([kblk, ones_cols], axis=1), qms[h])

    def key_group(first_kb, state, masked):
        carries, accs = list(state[:nh]), list(state[nh:])
        order = [(u, h) for u in range(n_diag) for h in range(nh)]
        zs, mids, rs, ws = {}, {}, {}, {}

        def stage_logits(c):
            u, h = order[c]
            zs[c] = logits(first_kb + (n_diag - 1 - u), h)

        def stage_scan(c):
            u, h = order[c]
            cols = slice(tq // 2, tq) if (masked and u == 0) else slice(0, tq)
            z = zs.pop(c)[:, cols]
            drop = jnp.maximum(z, 0.0) + jnp.log(1.0 + jnp.exp2(_neg_abs(z))) * LOG2E
            valid = None
            if masked:
                valid = diag_valid[u][:, cols]
                drop = jnp.where(valid, drop, 0.0)
            hi, lo = _split2(drop)
            mids[c] = (z, valid, cols)
            rs[c] = _dot(scan, jnp.concatenate([hi, lo], axis=0))

        def stage_values(c):
            u, h = order[c]
            z, valid, cols = mids.pop(c)
            r = rs.pop(c)
            n_cols = cols.stop - cols.start
            w = jnp.exp2((z - r[:tk]).reshape(tk // 8, 8, n_cols) - carries[h][:, cols]
                         ).reshape(tk, n_cols)
            if masked:
                w = jnp.where(valid, w, 0.0)
            total = r[tk:tk + 8]
            if n_cols < tq:
                w = jnp.concatenate([jnp.zeros((tk, tq - n_cols), F32), w], axis=1)
                total = jnp.concatenate([jnp.zeros((8, tq - n_cols), F32), total], axis=1)
            ws.setdefault(h, []).append(w.astype(BF16))
            carries[h] = carries[h] + total
            if u == n_diag - 1:
                v_cat = jnp.concatenate(
                    [vT_ref[first_kb + (n_diag - 1 - uu), h * hd:(h + 1) * hd, :]
                     for uu in range(n_diag)], axis=1)
                accs[h] = accs[h] + _dot(v_cat, jnp.concatenate(ws.pop(h), axis=0))

        n_chains = len(order)
        for s in range(n_chains + 2 * SB_CHAIN_SKEW):
            if s < n_chains:
                stage_logits(s)
            if 0 <= s - SB_CHAIN_SKEW < n_chains:
                stage_scan(s - SB_CHAIN_SKEW)
            if 0 <= s - 2 * SB_CHAIN_SKEW < n_chains:
                stage_values(s - 2 * SB_CHAIN_SKEW)
        return tuple(carries) + tuple(accs)

    state = tuple([jnp.zeros((8, tq), F32)] * nh + [jnp.zeros((hd, tq), F32)] * nh)
    state = key_group(n_diag * i, state, True)
    state = lax.fori_loop(0, i, lambda t, st: key_group(n_diag * (i - 1 - t), st, False), state)

    oT = jnp.concatenate(state[nh:], axis=0)
    o_ref[...] = oT.T.astype(o_ref.dtype)


def _sb_prompt(bias, qT3, k_bf, vT3, batch, seq_len):
    n = k_bf.shape[0]
    tq, tk = SB_Q_TILE, SB_K_TILE
    assert seq_len % tq == 0
    qt = seq_len // tq
    kblocks = seq_len // tk
    width = SB_HEADS_PER_STEP * SB_HEAD_DIM
    scan = _sb_scan_matrix()
    return pl.pallas_call(
        _sb_prompt_kernel,
        grid=(batch, SB_HEADS // SB_HEADS_PER_STEP, qt),
        in_specs=[pl.BlockSpec(memory_space=pltpu.SMEM),
                  pl.BlockSpec((1, width, tq), lambda b, hg, i: (b * qt + i, hg, 0)),
                  pl.BlockSpec((seq_len, width), lambda b, hg, i: (b, hg)),
                  pl.BlockSpec((kblocks, width, tk), lambda b, hg, i: (b, hg, 0)),
                  _const_spec(scan.shape)],
        out_specs=pl.BlockSpec((tq, width), lambda b, hg, i: (b * qt + i, hg)),
        out_shape=jax.ShapeDtypeStruct((n, SB_WIDTH), BF16),
        compiler_params=_cparams("parallel", "parallel", "arbitrary"),
        name="sb_prompt",
    )(bias, qT3, k_bf, vT3, scan)


def _gla_consts():
    t = np.arange(GLA_BLOCK)
    same = (t[:, None] // GLA_CHUNK) == (t[None, :] // GLA_CHUNK)
    tri = (same & (t[None, :] <= t[:, None])).astype(np.float32)
    ones = same.astype(np.float32)
    scan = np.concatenate([np.concatenate([tri] * 3, axis=1),
                           np.concatenate([ones] * 3, axis=1)], axis=0)
    n_chunks = GLA_BLOCK // GLA_CHUNK
    ind = np.zeros((GLA_BLOCK, n_chunks * LANES), np.float32)
    for c in range(n_chunks):
        ind[c * GLA_CHUNK:(c + 1) * GLA_CHUNK, c * LANES:(c + 1) * LANES] = 1.0
    ind = np.concatenate([ind] * 3, axis=0)
    return jnp.asarray(scan, dtype=BF16), jnp.asarray(ind, dtype=BF16)


def _gla_prompt_kernel(gq_ref, gk_ref, gv_ref, la_ref, gr_ref, ng_ref, scan_ref, ind_ref,
                       og_ref, st_ref, s_scr):
    j = pl.program_id(1)
    tb, c_len = GLA_BLOCK, GLA_CHUNK
    n_chunks = tb // c_len
    dk, dv, nh = GLA_HEAD_K, GLA_HEAD_V, GLA_HEADS

    @pl.when(j == 0)
    def _():
        s_scr[...] = jnp.zeros_like(s_scr)

    la = la_ref[...]
    parts = jnp.concatenate(_split3(la), axis=0)
    bb = _dot(scan_ref[...], parts)
    b = bb[:tb]
    b_last = bb[tb:]
    laT_parts = jnp.concatenate(_split3(la.T), axis=1)
    b_last_col = _dot(laT_parts, ind_ref[...])

    q = gq_ref[...] * (dk ** -0.5)
    k = gk_ref[...]
    v = gv_ref[...]
    q_dec = q * jnp.exp(b)
    k_inv = (k * jnp.exp(-b)).astype(BF16)
    k_endT = (k * jnp.exp(b_last - b)).T

    lane_head = _div_pow2(lax.broadcasted_iota(jnp.int32, (tb, nh * dk), 1), dk)
    ti = lax.broadcasted_iota(jnp.int32, (tb, tb), 0)
    tj = lax.broadcasted_iota(jnp.int32, (tb, tb), 1)
    tj_chunk = _div_pow2(tj, c_len)
    causal = (_div_pow2(ti, c_len) == tj_chunk) & (tj <= ti)

    qm = [jnp.where(lane_head == h, q_dec, 0.0).astype(BF16) for h in range(nh)]
    scores = [_dot_nt(qm[h], k_inv) for h in range(nh)]
    us = []
    for c in range(n_chunks):
        k_endT_c = jnp.where(tj_chunk == c, k_endT, 0.0).astype(BF16)
        us.append(_dot(k_endT_c, v))
    o_intra = [_dot(jnp.where(causal, scores[h], 0.0).astype(BF16), v[:, h * dv:(h + 1) * dv])
               for h in range(nh)]

    states = [s_scr[...]]
    for c in range(n_chunks):
        u_diag = jnp.concatenate(
            [us[c][h * dk:(h + 1) * dk, h * dv:(h + 1) * dv] for h in range(nh)], axis=0)
        states.append(jnp.exp(b_last_col[:, c * LANES:(c + 1) * LANES]) * states[c] + u_diag)
    o_inter = []
    for c in range(n_chunks):
        rows = slice(c * c_len, (c + 1) * c_len)
        q_stack = jnp.concatenate([qm[h][rows] for h in range(nh)], axis=0)
        o_inter.append(_dot(q_stack, states[c].astype(BF16)))

    norm_g = ng_ref[...]
    for c in range(n_chunks):
        rows = slice(c * c_len, (c + 1) * c_len)
        for h in range(nh):
            o = o_intra[h][rows] + o_inter[c][h * c_len:(h + 1) * c_len]
            o = o * lax.rsqrt(jnp.mean(o * o, axis=-1, keepdims=True) + GN_EPS) * norm_g
            r = gr_ref[rows, h * dv:(h + 1) * dv]
            og_ref[rows, h * dv:(h + 1) * dv] = (o * (r * _sigmoid(r))).astype(og_ref.dtype)
    s_all = states[n_chunks]
    s_scr[...] = s_all

    @pl.when(j == pl.num_programs(1) - 1)
    def _():
        st_ref[0] = s_all


def _gla_prompt(gq, gk, gv, la, gr, norm_g, batch, seq_len):
    n = gq.shape[0]
    tb = GLA_BLOCK
    assert seq_len % tb == 0
    nb = seq_len // tb
    scan, ind = _gla_consts()
    row = lambda cols: pl.BlockSpec((tb, cols), lambda b, j: (b * nb + j, 0))
    rows_s = GLA_HEADS * GLA_HEAD_K
    return pl.pallas_call(
        _gla_prompt_kernel,
        grid=(batch, nb),
        in_specs=[row(GLA_K_WIDTH), row(GLA_K_WIDTH), row(GLA_V_WIDTH), row(GLA_K_WIDTH),
                  row(GLA_V_WIDTH), _const_spec((1, GLA_HEAD_V)),
                  _const_spec(scan.shape), _const_spec(ind.shape)],
        out_specs=(row(GLA_V_WIDTH),
                   pl.BlockSpec((1, rows_s, GLA_HEAD_V), lambda b, j: (b, 0, 0))),
        out_shape=(jax.ShapeDtypeStruct((n, GLA_V_WIDTH), BF16),
                   jax.ShapeDtypeStruct((batch, rows_s, GLA_HEAD_V), F32)),
        scratch_shapes=[pltpu.VMEM((rows_s, GLA_HEAD_V), F32)],
        compiler_params=_cparams("parallel", "arbitrary"),
        name="gla_prompt",
    )(gq, gk, gv, la, gr, norm_g.reshape(1, GLA_HEAD_V), scan, ind)


POST_SEGMENTS = 8


def _post_math(alpha, x_ref, osb_ref, og_ref, mod_ref, wg_ref, wua_ref, wub_ref, wo_ref,
               ln1g_ref, ln1b_ref, wf1_ref, wf2_ref, ln2g_ref, ln2b_ref, y_ref, segment):
    x = x_ref[...]
    sh1, sc1, gt1, sh2, sc2, gt2 = [mod_ref[0, m] for m in range(6)]
    d = x.shape[-1]
    assert sum(FF_CHUNKS) == wf1_ref.shape[1] and POST_SEGMENTS == 3 + len(FF_CHUNKS)

    def seg0():
        h = (_ln_stats(x) * (1.0 + sc1) + sh1).astype(BF16)
        return h, _sigmoid(_dot(h, wg_ref[:, :d]))

    h, gate_a = segment(0, seg0)
    part_a, gate_b = segment(1, lambda: (gate_a * _dot(osb_ref[...].astype(BF16), wua_ref[...]),
                                         _sigmoid(_dot(h, wg_ref[:, d:]))))

    def seg2():
        merged = (part_a + gate_b * _dot(og_ref[...].astype(BF16), wub_ref[...])).astype(BF16)
        x1 = _ln_stats(alpha * x + gt1 * _dot(merged, wo_ref[...])) * ln1g_ref[...] + ln1b_ref[...]
        return x1, (_ln_stats(x1) * (1.0 + sc2) + sh2).astype(BF16)

    x1, h2 = segment(2, seg2)
    f = jnp.zeros_like(x)
    lo = 0
    for c, width in enumerate(FF_CHUNKS):
        cols = slice(lo, lo + width)
        lo += width

        def ff(f=f, cols=cols, last=(c == len(FF_CHUNKS) - 1)):
            u = jnp.maximum(_dot(h2, wf1_ref[:, cols]), 0.0)
            f_new = f + _dot((u * u).astype(BF16), wf2_ref[cols, :])
            if last:
                y_ref[...] = _ln_stats(alpha * x1 + gt2 * f_new) * ln2g_ref[...] + ln2b_ref[...]
            return f_new

        f = segment(3 + c, ff)


def _post_kernel(alpha, *refs):
    _post_math(alpha, *refs, segment=lambda j, fn: fn())


def _post(x2d, o_sb, o_g, mod, seq_len, alpha, wg, wua, wub, wo, ln1g, ln1b, wf1, wf2, ln2g, ln2b):
    n, d = x2d.shape
    per_token = mod.shape[2] != 1
    tm = n if per_token else min(TOKEN_TILE, seq_len)
    tiles_per_seq = 1 if per_token else seq_len // tm
    row = lambda cols: pl.BlockSpec((tm, cols), lambda i: (i, 0))
    single = lambda a: pl.BlockSpec(a.shape, lambda i: (0,) * a.ndim, pipeline_mode=pl.Buffered(1))
    vec = lambda a: a.reshape(1, d)
    mod_spec = (_const_spec(mod.shape) if per_token else
                pl.BlockSpec((1, 6, 1, d), lambda i: (i // tiles_per_seq, 0, 0, 0)))
    return pl.pallas_call(
        functools.partial(_post_kernel, alpha),
        grid=(n // tm,),
        in_specs=[row(d), row(SB_WIDTH), row(GLA_V_WIDTH), mod_spec,
                  single(wg), single(wua), single(wub), single(wo),
                  _const_spec((1, d)), _const_spec((1, d)),
                  single(wf1), single(wf2),
                  _const_spec((1, d)), _const_spec((1, d))],
        out_specs=row(d),
        out_shape=jax.ShapeDtypeStruct((n, d), F32),
        compiler_params=_cparams("parallel"),
        name="post",
    )(x2d, o_sb, o_g, mod, wg, wua, wub, wo, vec(ln1g), vec(ln1b), wf1, wf2, vec(ln2g), vec(ln2b))


def _lane_scan_matrix():
    j = np.arange(LANES)[:, None]
    s = np.arange(LANES)[None, :]
    later = (j > s).astype(np.float32)
    half = np.concatenate([later, np.ones((LANES, LANES), np.float32)], axis=1)
    return jnp.asarray(np.concatenate([half, half], axis=0), dtype=BF16)


def _paged_logits(k_pages, qb_scr, bias_ref, lb_scr, hl_scr):
    p, nh, hd = PAGED_PAGES, SB_HEADS, SB_HEAD_DIM
    sublane = lax.broadcasted_iota(jnp.int32, (nh, LANES), 0)
    zs = [jnp.zeros((nh, LANES), F32)] * p
    bias = jnp.zeros((nh, LANES), F32)
    for h in range(nh):
        hrows = slice(h * hd, (h + 1) * hd)
        qh = qb_scr[hrows, :]
        bias = jnp.where(sublane == h, bias_ref[h], bias)
        for s in range(p):
            tot = jnp.sum(k_pages[s, hrows, :] * qh, axis=0, keepdims=True)
            zs[s] = jnp.where(sublane == h, tot, zs[s])
    z = jnp.concatenate([zs[s] + bias for s in range(p)], axis=0)
    soft = jnp.log1p(jnp.exp(-jnp.abs(z)))
    log_beta = jnp.minimum(z, 0.0) - soft
    hi, lo = _split2(log_beta - z)
    lb_scr[...] = log_beta
    hl_scr[...] = jnp.concatenate([hi, lo], axis=1)


def _paged_accumulate(v_pages, r1, first, lb_scr, carry_scr, acc_scr):
    p, nh, hd = PAGED_PAGES, SB_HEADS, SB_HEAD_DIM
    carry = jnp.where(first, 0.0, carry_scr[...])
    row_tot = r1[:, LANES:]
    prefix, run = [], carry
    for s in range(p):
        prefix.append(run)
        run = run + row_tot[s * nh:(s + 1) * nh]
    carry_scr[...] = run
    w = jnp.exp(lb_scr[...] + r1[:, :LANES] + jnp.concatenate(prefix, axis=0))
    for h in range(nh):
        hrows = slice(h * hd, (h + 1) * hd)
        acc = jnp.where(first, 0.0, acc_scr[hrows, :])
        for s in range(p):
            acc = acc + v_pages[s, hrows, :] * w[s * nh + h:s * nh + h + 1, :]
        acc_scr[hrows, :] = acc


def _post_paged_kernel(alpha, n_pages, pt_ref, bias_ref, *refs):
    post_refs = refs[:14]
    q_ref, lscan_ref, kc_hbm, vc_hbm = refs[14:18]
    y_ref, os_ref = refs[18:20]
    kbuf, vbuf, ksem, vsem, carry_scr, acc_scr, qb_scr, lb_scr, hl_scr = refs[20:]
    p, subs, slots = PAGED_PAGES, POST_SEGMENTS, PAGE_SLOTS
    groups = n_pages // p
    i = pl.program_id(0)
    n_steps = pl.num_programs(0)

    def slot_of(t):
        return lax.rem(t + slots, slots)

    def copies(hbm, buf, sem, t):
        b, g, slot = lax.div(t, groups), lax.rem(t, groups), slot_of(t)
        return [pltpu.make_async_copy(hbm.at[pt_ref[b, n_pages - 1 - (g * p + s)]],
                                      buf.at[slot, s], sem.at[slot]) for s in range(p)]

    def start(cs):
        for c in cs:
            c.start()

    def wait(cs):
        for c in cs:
            c.wait()

    def finish_sequence(b):
        acc_t = acc_scr[...].T
        os_ref[pl.ds(b, 1), :] = jnp.sum(acc_t, axis=0, keepdims=True)

    scanned = {}

    def before(j):
        t = subs * i + j
        if j == 0:
            @pl.when(i == 0)
            def _():
                hl_scr[...] = jnp.zeros_like(hl_scr)
                lb_scr[...] = jnp.zeros_like(lb_scr)
                carry_scr[...] = jnp.zeros_like(carry_scr)
                acc_scr[...] = jnp.zeros_like(acc_scr)
                vbuf[slots - 1] = jnp.zeros(vbuf.shape[1:], vbuf.dtype)
                start(copies(kc_hbm, kbuf, ksem, t))
                start(copies(kc_hbm, kbuf, ksem, t + 1))
                start(copies(vc_hbm, vbuf, vsem, t))

            @pl.when(i > 0)
            def _():
                wait(copies(vc_hbm, vbuf, vsem, t - 1))
        else:
            wait(copies(vc_hbm, vbuf, vsem, t - 1))
        wait(copies(kc_hbm, kbuf, ksem, t))
        for ahead, hbm, buf, sem in ((2, kc_hbm, kbuf, ksem), (1, vc_hbm, vbuf, vsem)):
            if j + ahead < subs:
                start(copies(hbm, buf, sem, t + ahead))
            else:
                @pl.when(i + 1 < n_steps)
                def _(ahead=ahead, hbm=hbm, buf=buf, sem=sem):
                    start(copies(hbm, buf, sem, t + ahead))
        scanned[j] = _dot(hl_scr[...], lscan_ref[...])

    def after(j):
        t = subs * i + j
        first = lax.rem(t - 1 + groups, groups) == 0
        _paged_accumulate(vbuf.at[slot_of(t - 1)], scanned[j], first, lb_scr, carry_scr, acc_scr)
        if j == 0:
            @pl.when(jnp.logical_and(lax.rem(t, groups) == 0, t > 0))
            def _():
                finish_sequence(lax.div(t, groups) - 1)

            @pl.when(lax.rem(t, groups) == 0)
            def _():
                q_row = q_ref[pl.ds(lax.div(t, groups), 1), :]
                qb_scr[...] = jnp.broadcast_to(q_row, (LANES, q_row.shape[1])).T
        _paged_logits(kbuf.at[slot_of(t)], qb_scr, bias_ref, lb_scr, hl_scr)

    def segment(j, fn):
        before(j)
        out = fn()
        after(j)
        return out

    _post_math(alpha, *post_refs, y_ref, segment=segment)

    @pl.when(i == n_steps - 1)
    def _():
        t_last = subs * n_steps - 1
        wait(copies(vc_hbm, vbuf, vsem, t_last))
        r1 = _dot(hl_scr[...], lscan_ref[...])
        _paged_accumulate(vbuf.at[slot_of(t_last)], r1, False, lb_scr, carry_scr, acc_scr)
        finish_sequence(lax.div(t_last, groups))


def _post_with_paged_attention(x2d, o_sb, o_g, mod, seq_len, alpha, post_w,
                               page_table, bias, q_s, cache_kT, cache_vT):
    wg, wua, wub, wo, ln1g, ln1b, wf1, wf2, ln2g, ln2b = post_w
    n, d = x2d.shape
    nseq, n_pages = page_table.shape
    p, width = PAGED_PAGES, SB_WIDTH
    tm = min(TOKEN_TILE, seq_len)
    tiles_per_seq = seq_len // tm
    steps = n // tm
    assert cache_kT.shape[1:] == (width, LANES) and n_pages % p == 0
    assert (n_pages // p) % POST_SEGMENTS == 0 and steps * POST_SEGMENTS * p == nseq * n_pages
    lscan = _lane_scan_matrix()

    row = lambda cols: pl.BlockSpec((tm, cols), lambda i, pt: (i, 0))
    const = lambda shape: pl.BlockSpec(shape, lambda i, pt: (0,) * len(shape))
    single = lambda a: pl.BlockSpec(a.shape, lambda i, pt: (0,) * a.ndim,
                                    pipeline_mode=pl.Buffered(1))
    vec = lambda a: a.reshape(1, d)
    hbm = pl.BlockSpec(memory_space=pl.ANY)
    grid_spec = pltpu.PrefetchScalarGridSpec(
        num_scalar_prefetch=1,
        grid=(steps,),
        in_specs=[pl.BlockSpec(memory_space=pltpu.SMEM),
                  row(d), row(SB_WIDTH), row(GLA_V_WIDTH),
                  pl.BlockSpec((1, 6, 1, d), lambda i, pt: (i // tiles_per_seq, 0, 0, 0)),
                  single(wg), single(wua), single(wub), single(wo),
                  const((1, d)), const((1, d)), single(wf1), single(wf2),
                  const((1, d)), const((1, d)),
                  const(q_s.shape), const(lscan.shape), hbm, hbm],
        out_specs=(row(d), const((nseq, width))),
        scratch_shapes=[pltpu.VMEM((PAGE_SLOTS, p, width, LANES), F32),
                        pltpu.VMEM((PAGE_SLOTS, p, width, LANES), F32),
                        pltpu.SemaphoreType.DMA((PAGE_SLOTS,)),
                        pltpu.SemaphoreType.DMA((PAGE_SLOTS,)),
                        pltpu.VMEM((SB_HEADS, LANES), F32),
                        pltpu.VMEM((width, LANES), F32),
                        pltpu.VMEM((width, LANES), F32),
                        pltpu.VMEM((p * SB_HEADS, LANES), F32),
                        pltpu.VMEM((p * SB_HEADS, 2 * LANES), BF16)],
    )
    return pl.pallas_call(
        functools.partial(_post_paged_kernel, alpha, n_pages),
        grid_spec=grid_spec,
        out_shape=(jax.ShapeDtypeStruct((n, d), F32), jax.ShapeDtypeStruct((nseq, width), F32)),
        compiler_params=_cparams("arbitrary"),
        name="post_paged",
    )(page_table, bias, x2d, o_sb, o_g, mod, wg, wua, wub, wo, vec(ln1g), vec(ln1b), wf1, wf2,
      vec(ln2g), vec(ln2b), q_s, lscan, cache_kT, cache_vT)


def _gla_sample_kernel(gq_ref, gk_ref, la_ref, gv_ref, gr_ref, ng_ref, s0_ref, og_ref, st_ref):
    nh, dk, dv = GLA_HEADS, GLA_HEAD_K, GLA_HEAD_V
    la = la_ref[0]
    q_dec = gq_ref[0] * (dk ** -0.5) * jnp.exp(la)
    k_inv = gk_ref[0] * jnp.exp(-la)
    k_end = gk_ref[0]
    s0 = s0_ref[0]
    v8 = gv_ref[0]
    qk = q_dec * k_inv
    qs = q_dec * s0
    outs = []
    for h in range(nh):
        rows = slice(h * dk, (h + 1) * dk)
        att = jnp.sum(qk[rows], axis=0, keepdims=True)
        outs.append(att * v8[h:h + 1] + jnp.sum(qs[rows], axis=0, keepdims=True))
    o = jnp.concatenate(outs + [jnp.zeros((8 - nh, dv), F32)], axis=0)
    o = o * lax.rsqrt(jnp.mean(o * o, axis=-1, keepdims=True) + GN_EPS) * ng_ref[...]
    r = gr_ref[0]
    og_ref[0] = o * (r * _sigmoid(r))
    v_rows = jnp.concatenate(
        [jnp.broadcast_to(v8[h:h + 1], (dk, dv)) for h in range(nh)], axis=0)
    st_ref[0] = jnp.exp(la) * s0 + k_end * v_rows


def _gla_sample(gq, gk, gv, la, gr, norm_g, s0):
    n = gq.shape[0]
    nh, dk, dv = GLA_HEADS, GLA_HEAD_K, GLA_HEAD_V
    kw = nh * dk
    pad_heads = lambda a: jnp.pad(a.reshape(n, nh, dv), ((0, 0), (0, 8 - nh), (0, 0)))
    colspec = pl.BlockSpec((1, kw, 1), lambda b: (b, 0, 0))
    headspec = pl.BlockSpec((1, 8, dv), lambda b: (b, 0, 0))
    statespec = pl.BlockSpec((1, kw, dv), lambda b: (b, 0, 0))
    og, st = pl.pallas_call(
        _gla_sample_kernel,
        grid=(n,),
        in_specs=[colspec, colspec, colspec, headspec, headspec, _const_spec((1, dv)), statespec],
        out_specs=(headspec, statespec),
        out_shape=(jax.ShapeDtypeStruct((n, 8, dv), F32),
                   jax.ShapeDtypeStruct((n, kw, dv), F32)),
        compiler_params=_cparams("parallel"),
        name="gla_sample",
    )(gq.reshape(n, kw, 1), gk.reshape(n, kw, 1), la.reshape(n, kw, 1),
      pad_heads(gv), pad_heads(gr), norm_g.reshape(1, dv), s0)
    return og[:, :nh].reshape(n, nh * dv), st


def kernel(x_prompt, x_sample, cache_k, cache_v, state_gla, page_table, c_prompt, c_sample,
           w_ada, b_ada, w_in, sb_bias, w_gla_g2, b_gla_g, gla_norm_g, w_up_a, w_up_b, w_o,
           ln1_g, ln1_b, w_ff1, w_ff2, ln2_g, ln2_b):
    depth = w_ada.shape[0]
    pb, seq_len, d = x_prompt.shape
    sb_n = x_sample.shape[0]
    assert x_sample.shape[1] == 1
    alpha = (2.0 * depth) ** 0.25
    n_phys, page_size = cache_k.shape[1], cache_k.shape[2]

    yp = x_prompt.reshape(pb * seq_len, d)
    ys = x_sample.reshape(sb_n, d)
    c_all = jnp.concatenate([c_prompt, c_sample], axis=0)
    main_w = 3 * SB_WIDTH + 2 * GLA_K_WIDTH + 2 * GLA_V_WIDTH
    gate_off = main_w + GLA_GATE_RANK

    outs = [[] for _ in range(6)]
    for l in range(depth):
        ada = _ada(c_all, w_ada[l].astype(BF16), b_ada[l])
        mod_p = ada[:pb].reshape(pb, 6, 1, d)
        mod_s = ada[pb:].reshape(sb_n, 6, d).transpose(1, 0, 2).reshape(1, 6, sb_n, d)

        w_in_l = w_in[l]
        wm = w_in_l[:, :main_w].astype(BF16)
        wl = jnp.pad(w_in_l[:, main_w:gate_off], ((0, 0), (0, LANES - GLA_GATE_RANK))).astype(BF16)
        wg = w_in_l[:, gate_off:].astype(BF16)
        wg2 = jnp.pad(w_gla_g2[l], ((0, LANES - GLA_GATE_RANK), (0, 0))).astype(BF16)
        bg = b_gla_g[l].reshape(1, GLA_K_WIDTH)
        post_w = (wg, w_up_a[l].astype(BF16), w_up_b[l].astype(BF16), w_o[l].astype(BF16),
                  ln1_g[l], ln1_b[l], w_ff1[l].astype(BF16), w_ff2[l].astype(BF16),
                  ln2_g[l], ln2_b[l])

        qT3, k_p, v_p, k_bf, vT3, gq, gk, gv, gr, la = _proj_prompt(yp, mod_p, seq_len, wm, wl, wg2, bg)
        q_s, k_s, v_s, gq_s, gk_s, gv_s, gr_s, la_s = _proj_sample(ys, mod_s, wm, wl, wg2, bg)
        o_sb = _sb_prompt(sb_bias[l], qT3, k_bf, vT3, pb, seq_len)
        o_g, st_p = _gla_prompt(gq, gk, gv, la, gr, gla_norm_g[l], pb, seq_len)

        page_major = lambda c: jnp.transpose(c, (0, 2, 3, 1)).reshape(n_phys, SB_WIDTH, page_size)
        yp, o_sb_s = _post_with_paged_attention(
            yp, o_sb, o_g, mod_p, seq_len, alpha, post_w, page_table, sb_bias[l], q_s,
            page_major(cache_k[l]), page_major(cache_v[l]))

        s0 = state_gla[l].reshape(sb_n, GLA_HEADS * GLA_HEAD_K, GLA_HEAD_V)
        o_g_s, st_s = _gla_sample(gq_s, gk_s, gv_s, la_s, gr_s, gla_norm_g[l], s0)
        ys = _post(ys, o_sb_s, o_g_s, mod_s, 1, alpha, *post_w)

        rows_major = lambda t: jnp.transpose(
            t.reshape(pb, SB_HEADS, SB_HEAD_DIM, seq_len), (0, 3, 1, 2))
        outs[0].append(rows_major(k_p))
        outs[1].append(rows_major(v_p))
        outs[2].append(st_p.reshape(pb, GLA_HEADS, GLA_HEAD_K, GLA_HEAD_V))
        outs[3].append(k_s.reshape(sb_n, 1, SB_HEADS, SB_HEAD_DIM))
        outs[4].append(v_s.reshape(sb_n, 1, SB_HEADS, SB_HEAD_DIM))
        outs[5].append(st_s.reshape(sb_n, GLA_HEADS, GLA_HEAD_K, GLA_HEAD_V))

    stacked = [jnp.stack(o) for o in outs]
    return (yp.reshape(pb, seq_len, d), ys.reshape(sb_n, 1, d),
            stacked[0], stacked[1], stacked[2].astype(state_gla.dtype),
            stacked[3], stacked[4], stacked[5].astype(state_gla.dtype))
```

```python
import functools

import numpy as np
import jax
import jax.numpy as jnp
from jax import lax
from jax.experimental import pallas as pl
from jax.experimental.pallas import tpu as pltpu

F32 = jnp.float32
BF16 = jnp.bfloat16

SB_HEADS = 8
SB_HEAD_DIM = 64
SB_WIDTH = SB_HEADS * SB_HEAD_DIM
GLA_HEADS = 4
GLA_HEAD_K = 64
GLA_HEAD_V = 128
GLA_K_WIDTH = GLA_HEADS * GLA_HEAD_K
GLA_V_WIDTH = GLA_HEADS * GLA_HEAD_V
GLA_GATE_RANK = 16
GLA_TAU = 16.0
GLA_CHUNK = 64
LN_EPS = 1e-5
GN_EPS = 1e-6

LANES = 128
VMEM_LIMIT_BYTES = 56 * 1024 * 1024

TOKEN_TILE = 512
PROJ_TOKEN_TILE = 1024
SB_Q_TILE = 256
SB_K_TILE = 128
SB_CHAIN_SKEW = 10
SB_HEADS_PER_STEP = 8
GLA_BLOCK = 256
PAGED_PAGES = 8
FF_CHUNKS = (768, 768, 768, 768, 1024)
PAGE_SLOTS = 3


def _cparams(*sem):
    return pltpu.CompilerParams(dimension_semantics=sem, vmem_limit_bytes=VMEM_LIMIT_BYTES)


def _dot(a, b):
    return jnp.dot(a, b, preferred_element_type=F32)


def _dot_nt(a, b):
    return lax.dot_general(a, b, (((1,), (1,)), ((), ())), preferred_element_type=F32)


def _ln_stats(x):
    mu = jnp.mean(x, axis=-1, keepdims=True)
    xc = x - mu
    var = jnp.mean(xc * xc, axis=-1, keepdims=True)
    return xc * lax.rsqrt(var + LN_EPS)


def _log_sigmoid(x):
    return jnp.minimum(x, 0.0) - jnp.log1p(jnp.exp(-jnp.abs(x)))


def _sigmoid(x):
    return 1.0 / (1.0 + jnp.exp(-x))


LOG2E = 1.4426950408889634


def _neg_abs(x):
    bits = lax.bitcast_convert_type(x, jnp.int32) | jnp.int32(-2 ** 31)
    return lax.bitcast_convert_type(bits, F32)


def _div_pow2(x, n):
    assert n & (n - 1) == 0
    return jnp.right_shift(x, n.bit_length() - 1)


def _split2(x):
    hi = x.astype(BF16)
    lo = (x - hi.astype(F32)).astype(BF16)
    return hi, lo


def _split3(x):
    hi = x.astype(BF16)
    r = x - hi.astype(F32)
    mid = r.astype(BF16)
    lo = (r - mid.astype(F32)).astype(BF16)
    return hi, mid, lo


def _ada_kernel(c_ref, w_ref, b_ref, o_ref):
    c = c_ref[...]
    a = c * _sigmoid(c)
    o_ref[...] = _dot(a.astype(BF16), w_ref[...]) + b_ref[...]


def _ada(c_all, w_ada_b, b_ada):
    n, d = c_all.shape
    nout = w_ada_b.shape[1]
    blk = d
    return pl.pallas_call(
        _ada_kernel,
        grid=(nout // blk,),
        in_specs=[pl.BlockSpec((n, d), lambda j: (0, 0)),
                  pl.BlockSpec((d, blk), lambda j: (0, j)),
                  pl.BlockSpec((1, blk), lambda j: (0, j))],
        out_specs=pl.BlockSpec((n, blk), lambda j: (0, j)),
        out_shape=jax.ShapeDtypeStruct((n, nout), F32),
        compiler_params=_cparams("parallel"),
        name="ada",
    )(c_all, w_ada_b, b_ada.reshape(1, nout))


def _proj_math(x, sh1, sc1, wm_ref, wl_ref, wg2_ref, bg_ref):
    h = (_ln_stats(x) * (1.0 + sc1) + sh1).astype(BF16)
    y = _dot(h, wm_ref[...])
    g_low = _dot(h, wl_ref[...])
    gl = _dot(g_low.astype(BF16), wg2_ref[...]) + bg_ref[...]
    log_a = _log_sigmoid(gl) * (1.0 / GLA_TAU)
    return y, log_a


def _proj_prompt_kernel(x_ref, mod_ref, wm_ref, wl_ref, wg2_ref, bg_ref,
                        qT_ref, kT_ref, vT_ref, kb_ref, vTb_ref, gq_ref, gk_ref, gv_ref, gr_ref, la_ref):
    y, log_a = _proj_math(x_ref[...], mod_ref[0, 0], mod_ref[0, 1], wm_ref, wl_ref, wg2_ref, bg_ref)
    w = SB_WIDTH
    q = y[:, 0:w] * (SB_HEAD_DIM ** -0.5 * LOG2E)
    k = y[:, w:2 * w]
    v = y[:, 2 * w:3 * w]
    kb_ref[...] = k.astype(BF16)
    qT = q.T.astype(BF16)
    vT = v.T
    kT_ref[0] = k.T
    vT_ref[0] = vT
    vT = vT.astype(BF16)
    for c in range(qT_ref.shape[0]):
        qT_ref[c] = qT[:, c * SB_Q_TILE:(c + 1) * SB_Q_TILE]
    for c in range(vTb_ref.shape[0]):
        vTb_ref[c] = vT[:, c * SB_K_TILE:(c + 1) * SB_K_TILE]
    o = 3 * w
    gq_ref[...] = y[:, o:o + GLA_K_WIDTH]
    gk_ref[...] = y[:, o + GLA_K_WIDTH:o + 2 * GLA_K_WIDTH]
    o += 2 * GLA_K_WIDTH
    gv_ref[...] = y[:, o:o + GLA_V_WIDTH].astype(BF16)
    gr_ref[...] = y[:, o + GLA_V_WIDTH:o + 2 * GLA_V_WIDTH]
    la_ref[...] = log_a


def _proj_sample_kernel(x_ref, mod_ref, wm_ref, wl_ref, wg2_ref, bg_ref,
                        q_ref, k_ref, v_ref, gq_ref, gk_ref, gv_ref, gr_ref, la_ref):
    y, log_a = _proj_math(x_ref[...], mod_ref[0, 0], mod_ref[0, 1], wm_ref, wl_ref, wg2_ref, bg_ref)
    w = SB_WIDTH
    q_ref[...] = y[:, 0:w] * (SB_HEAD_DIM ** -0.5)
    k_ref[...] = y[:, w:2 * w]
    v_ref[...] = y[:, 2 * w:3 * w]
    o = 3 * w
    gq_ref[...] = y[:, o:o + GLA_K_WIDTH]
    gk_ref[...] = y[:, o + GLA_K_WIDTH:o + 2 * GLA_K_WIDTH]
    o += 2 * GLA_K_WIDTH
    gv_ref[...] = y[:, o:o + GLA_V_WIDTH]
    gr_ref[...] = y[:, o + GLA_V_WIDTH:o + 2 * GLA_V_WIDTH]
    la_ref[...] = log_a


def _const_spec(shape):
    nd = len(shape)
    return pl.BlockSpec(shape, lambda *a: (0,) * nd)


def _proj_prompt(x2d, mod, seq_len, wm, wl, wg2, bg):
    n, d = x2d.shape
    tm = min(PROJ_TOKEN_TILE, seq_len)
    assert seq_len % tm == 0 and tm % SB_Q_TILE == 0
    tiles_per_seq = seq_len // tm
    row = lambda cols: pl.BlockSpec((tm, cols), lambda i: (i, 0))
    dims_major = pl.BlockSpec((1, SB_WIDTH, tm),
                              lambda i: (i // tiles_per_seq, 0, i % tiles_per_seq))
    out_shape = (
        jax.ShapeDtypeStruct((n // SB_Q_TILE, SB_WIDTH, SB_Q_TILE), BF16),
        jax.ShapeDtypeStruct((n // seq_len, SB_WIDTH, seq_len), F32),
        jax.ShapeDtypeStruct((n // seq_len, SB_WIDTH, seq_len), F32),
        jax.ShapeDtypeStruct((n, SB_WIDTH), BF16),
        jax.ShapeDtypeStruct((n // SB_K_TILE, SB_WIDTH, SB_K_TILE), BF16),
        jax.ShapeDtypeStruct((n, GLA_K_WIDTH), F32),
        jax.ShapeDtypeStruct((n, GLA_K_WIDTH), F32),
        jax.ShapeDtypeStruct((n, GLA_V_WIDTH), BF16),
        jax.ShapeDtypeStruct((n, GLA_V_WIDTH), F32),
        jax.ShapeDtypeStruct((n, GLA_K_WIDTH), F32),
    )
    out_specs = (
        pl.BlockSpec((tm // SB_Q_TILE, SB_WIDTH, SB_Q_TILE), lambda i: (i, 0, 0)),
        dims_major, dims_major, row(SB_WIDTH),
        pl.BlockSpec((tm // SB_K_TILE, SB_WIDTH, SB_K_TILE), lambda i: (i, 0, 0)),
        row(GLA_K_WIDTH), row(GLA_K_WIDTH), row(GLA_V_WIDTH), row(GLA_V_WIDTH), row(GLA_K_WIDTH),
    )
    return pl.pallas_call(
        _proj_prompt_kernel,
        grid=(n // tm,),
        in_specs=[row(d),
                  pl.BlockSpec((1, 6, 1, d), lambda i: (i // tiles_per_seq, 0, 0, 0)),
                  _const_spec(wm.shape), _const_spec(wl.shape), _const_spec(wg2.shape),
                  _const_spec(bg.shape)],
        out_specs=out_specs,
        out_shape=out_shape,
        compiler_params=_cparams("parallel"),
        name="proj_prompt",
    )(x2d, mod, wm, wl, wg2, bg)


def _proj_sample(x2d, mod, wm, wl, wg2, bg):
    n, d = x2d.shape
    full = lambda cols: pl.BlockSpec((n, cols), lambda i: (0, 0))
    widths = (SB_WIDTH, SB_WIDTH, SB_WIDTH, GLA_K_WIDTH, GLA_K_WIDTH, GLA_V_WIDTH, GLA_V_WIDTH,
              GLA_K_WIDTH)
    return pl.pallas_call(
        _proj_sample_kernel,
        grid=(1,),
        in_specs=[full(d), _const_spec(mod.shape),
                  _const_spec(wm.shape), _const_spec(wl.shape), _const_spec(wg2.shape),
                  _const_spec(bg.shape)],
        out_specs=tuple(full(c) for c in widths),
        out_shape=tuple(jax.ShapeDtypeStruct((n, c), F32) for c in widths),
        compiler_params=_cparams("arbitrary"),
        name="proj_sample",
    )(x2d, mod, wm, wl, wg2, bg)


def _sb_scan_matrix():
    k = SB_K_TILE
    s = np.arange(k)[:, None]
    j = np.arange(k)[None, :]
    upper = (j >= s).astype(np.float32)
    top = np.concatenate([upper, upper], axis=1)
    ones = np.ones((16, 2 * k), np.float32)
    return jnp.asarray(np.concatenate([top, ones], axis=0), dtype=BF16)


def _sb_prompt_kernel(bias_ref, qT_ref, k_ref, vT_ref, a_ref, o_ref):
    hg = pl.program_id(1)
    i = pl.program_id(2)
    tq, tk, hd, nh = SB_Q_TILE, SB_K_TILE, SB_HEAD_DIM, SB_HEADS_PER_STEP
    n_diag = tq // tk
    assert n_diag == 2
    scan = a_ref[...]
    head_row = lax.broadcasted_iota(jnp.int32, (2 * hd, tq), 0)
    key_in_blk = lax.broadcasted_iota(jnp.int32, (tk, tq), 0)
    q_in_tile = lax.broadcasted_iota(jnp.int32, (tk, tq), 1)
    diag_valid = [(key_in_blk + (n_diag - 1 - u) * tk) < q_in_tile for u in range(n_diag)]

    bias_row = lax.broadcasted_iota(jnp.int32, (2 * hd, tq), 0)
    ones_cols = jnp.where(lax.broadcasted_iota(jnp.int32, (tk, 2 * hd), 1) < 3, 1.0, 0.0
                          ).astype(BF16)
    qms = []
    for h in range(nh):
        pair = qT_ref[0, (h // 2) * 2 * hd:(h // 2 + 1) * 2 * hd, :].astype(F32)
        own = (head_row < hd) if h % 2 == 0 else (head_row >= hd)
        parts = _split3(jnp.full((2 * hd, tq), bias_ref[nh * hg + h] * LOG2E, F32))
        bias_tile = jnp.zeros((2 * hd, tq), F32)
        for r, part in enumerate(parts):
            bias_tile = jnp.where(bias_row == r, part.astype(F32), bias_tile)
        qms.append(jnp.concatenate([jnp.where(own, pair, 0.0), bias_tile], axis=0).astype(BF16))

    def logits(kb, h):
        lanes = slice((h // 2) * 2 * hd, (h // 2 + 1) * 2 * hd)
        kblk = k_ref[pl.ds(pl.multiple_of(kb * tk, tk), tk), lanes]
        return _dot(jnp.concatenate([kblk, ones_cols], axis=1), qms[h])

    def key_group(first_kb, state, masked):
        carries, accs = list(state[:nh]), list(state[nh:])
        order = [(u, h) for u in range(n_diag) for h in range(nh)]
        zs, mids, rs, ws = {}, {}, {}, {}

        def stage_logits(c):
            u, h = order[c]
            zs[c] = logits(first_kb + (n_diag - 1 - u), h)

        def stage_scan(c):
            u, h = order[c]
            cols = slice(tq // 2, tq) if (masked and u == 0) else slice(0, tq)
            z = zs.pop(c)[:, cols]
            drop = jnp.maximum(z, 0.0) + jnp.log(1.0 + jnp.exp2(_neg_abs(z))) * LOG2E
            valid = None
            if masked:
                valid = diag_valid[u][:, cols]
                drop = jnp.where(valid, drop, 0.0)
            hi, lo = _split2(drop)
            mids[c] = (z, valid, cols)
            rs[c] = _dot(scan, jnp.concatenate([hi, lo], axis=0))

        def stage_values(c):
            u, h = order[c]
            z, valid, cols = mids.pop(c)
            r = rs.pop(c)
            n_cols = cols.stop - cols.start
            w = jnp.exp2((z - r[:tk]).reshape(tk // 8, 8, n_cols) - carries[h][:, cols]
                         ).reshape(tk, n_cols)
            if masked:
                w = jnp.where(valid, w, 0.0)
            total = r[tk:tk + 8]
            if n_cols < tq:
                w = jnp.concatenate([jnp.zeros((tk, tq - n_cols), F32), w], axis=1)
                total = jnp.concatenate([jnp.zeros((8, tq - n_cols), F32), total], axis=1)
            ws.setdefault(h, []).append(w.astype(BF16))
            carries[h] = carries[h] + total
            if u == n_diag - 1:
                v_cat = jnp.concatenate(
                    [vT_ref[first_kb + (n_diag - 1 - uu), h * hd:(h + 1) * hd, :]
                     for uu in range(n_diag)], axis=1)
                accs[h] = accs[h] + _dot(v_cat, jnp.concatenate(ws.pop(h), axis=0))

        n_chains = len(order)
        for s in range(n_chains + 2 * SB_CHAIN_SKEW):
            if s < n_chains:
                stage_logits(s)
            if 0 <= s - SB_CHAIN_SKEW < n_chains:
                stage_scan(s - SB_CHAIN_SKEW)
            if 0 <= s - 2 * SB_CHAIN_SKEW < n_chains:
                stage_values(s - 2 * SB_CHAIN_SKEW)
        return tuple(carries) + tuple(accs)

    state = tuple([jnp.zeros((8, tq), F32)] * nh + [jnp.zeros((hd, tq), F32)] * nh)
    state = key_group(n_diag * i, state, True)
    state = lax.fori_loop(0, i, lambda t, st: key_group(n_diag * (i - 1 - t), st, False), state)

    oT = jnp.concatenate(state[nh:], axis=0)
    o_ref[...] = oT.T.astype(o_ref.dtype)


def _sb_prompt(bias, qT3, k_bf, vT3, batch, seq_len):
    n = k_bf.shape[0]
    tq, tk = SB_Q_TILE, SB_K_TILE
    assert seq_len % tq == 0
    qt = seq_len // tq
    kblocks = seq_len // tk
    width = SB_HEADS_PER_STEP * SB_HEAD_DIM
    scan = _sb_scan_matrix()
    return pl.pallas_call(
        _sb_prompt_kernel,
        grid=(batch, SB_HEADS // SB_HEADS_PER_STEP, qt),
        in_specs=[pl.BlockSpec(memory_space=pltpu.SMEM),
                  pl.BlockSpec((1, width, tq), lambda b, hg, i: (b * qt + i, hg, 0)),
                  pl.BlockSpec((seq_len, width), lambda b, hg, i: (b, hg)),
                  pl.BlockSpec((kblocks, width, tk), lambda b, hg, i: (b, hg, 0)),
                  _const_spec(scan.shape)],
        out_specs=pl.BlockSpec((tq, width), lambda b, hg, i: (b * qt + i, hg)),
        out_shape=jax.ShapeDtypeStruct((n, SB_WIDTH), BF16),
        compiler_params=_cparams("parallel", "parallel", "arbitrary"),
        name="sb_prompt",
    )(bias, qT3, k_bf, vT3, scan)


def _gla_consts():
    t = np.arange(GLA_BLOCK)
    same = (t[:, None] // GLA_CHUNK) == (t[None, :] // GLA_CHUNK)
    tri = (same & (t[None, :] <= t[:, None])).astype(np.float32)
    ones = same.astype(np.float32)
    scan = np.concatenate([np.concatenate([tri] * 3, axis=1),
                           np.concatenate([ones] * 3, axis=1)], axis=0)
    n_chunks = GLA_BLOCK // GLA_CHUNK
    ind = np.zeros((GLA_BLOCK, n_chunks * LANES), np.float32)
    for c in range(n_chunks):
        ind[c * GLA_CHUNK:(c + 1) * GLA_CHUNK, c * LANES:(c + 1) * LANES] = 1.0
    ind = np.concatenate([ind] * 3, axis=0)
    return jnp.asarray(scan, dtype=BF16), jnp.asarray(ind, dtype=BF16)


def _gla_prompt_kernel(gq_ref, gk_ref, gv_ref, la_ref, gr_ref, ng_ref, scan_ref, ind_ref,
                       og_ref, st_ref, s_scr):
    j = pl.program_id(1)
    tb, c_len = GLA_BLOCK, GLA_CHUNK
    n_chunks = tb // c_len
    dk, dv, nh = GLA_HEAD_K, GLA_HEAD_V, GLA_HEADS

    @pl.when(j == 0)
    def _():
        s_scr[...] = jnp.zeros_like(s_scr)

    la = la_ref[...]
    parts = jnp.concatenate(_split3(la), axis=0)
    bb = _dot(scan_ref[...], parts)
    b = bb[:tb]
    b_last = bb[tb:]
    laT_parts = jnp.concatenate(_split3(la.T), axis=1)
    b_last_col = _dot(laT_parts, ind_ref[...])

    q = gq_ref[...] * (dk ** -0.5)
    k = gk_ref[...]
    v = gv_ref[...]
    q_dec = q * jnp.exp(b)
    k_inv = (k * jnp.exp(-b)).astype(BF16)
    k_endT = (k * jnp.exp(b_last - b)).T

    lane_head = _div_pow2(lax.broadcasted_iota(jnp.int32, (tb, nh * dk), 1), dk)
    ti = lax.broadcasted_iota(jnp.int32, (tb, tb), 0)
    tj = lax.broadcasted_iota(jnp.int32, (tb, tb), 1)
    tj_chunk = _div_pow2(tj, c_len)
    causal = (_div_pow2(ti, c_len) == tj_chunk) & (tj <= ti)

    qm = [jnp.where(lane_head == h, q_dec, 0.0).astype(BF16) for h in range(nh)]
    scores = [_dot_nt(qm[h], k_inv) for h in range(nh)]
    us = []
    for c in range(n_chunks):
        k_endT_c = jnp.where(tj_chunk == c, k_endT, 0.0).astype(BF16)
        us.append(_dot(k_endT_c, v))
    o_intra = [_dot(jnp.where(causal, scores[h], 0.0).astype(BF16), v[:, h * dv:(h + 1) * dv])
               for h in range(nh)]

    states = [s_scr[...]]
    for c in range(n_chunks):
        u_diag = jnp.concatenate(
            [us[c][h * dk:(h + 1) * dk, h * dv:(h + 1) * dv] for h in range(nh)], axis=0)
        states.append(jnp.exp(b_last_col[:, c * LANES:(c + 1) * LANES]) * states[c] + u_diag)
    o_inter = []
    for c in range(n_chunks):
        rows = slice(c * c_len, (c + 1) * c_len)
        q_stack = jnp.concatenate([qm[h][rows] for h in range(nh)], axis=0)
        o_inter.append(_dot(q_stack, states[c].astype(BF16)))

    norm_g = ng_ref[...]
    for c in range(n_chunks):
        rows = slice(c * c_len, (c + 1) * c_len)
        for h in range(nh):
            o = o_intra[h][rows] + o_inter[c][h * c_len:(h + 1) * c_len]
            o = o * lax.rsqrt(jnp.mean(o * o, axis=-1, keepdims=True) + GN_EPS) * norm_g
            r = gr_ref[rows, h * dv:(h + 1) * dv]
            og_ref[rows, h * dv:(h + 1) * dv] = (o * (r * _sigmoid(r))).astype(og_ref.dtype)
    s_all = states[n_chunks]
    s_scr[...] = s_all

    @pl.when(j == pl.num_programs(1) - 1)
    def _():
        st_ref[0] = s_all


def _gla_prompt(gq, gk, gv, la, gr, norm_g, batch, seq_len):
    n = gq.shape[0]
    tb = GLA_BLOCK
    assert seq_len % tb == 0
    nb = seq_len // tb
    scan, ind = _gla_consts()
    row = lambda cols: pl.BlockSpec((tb, cols), lambda b, j: (b * nb + j, 0))
    rows_s = GLA_HEADS * GLA_HEAD_K
    return pl.pallas_call(
        _gla_prompt_kernel,
        grid=(batch, nb),
        in_specs=[row(GLA_K_WIDTH), row(GLA_K_WIDTH), row(GLA_V_WIDTH), row(GLA_K_WIDTH),
                  row(GLA_V_WIDTH), _const_spec((1, GLA_HEAD_V)),
                  _const_spec(scan.shape), _const_spec(ind.shape)],
        out_specs=(row(GLA_V_WIDTH),
                   pl.BlockSpec((1, rows_s, GLA_HEAD_V), lambda b, j: (b, 0, 0))),
        out_shape=(jax.ShapeDtypeStruct((n, GLA_V_WIDTH), BF16),
                   jax.ShapeDtypeStruct((batch, rows_s, GLA_HEAD_V), F32)),
        scratch_shapes=[pltpu.VMEM((rows_s, GLA_HEAD_V), F32)],
        compiler_params=_cparams("parallel", "arbitrary"),
        name="gla_prompt",
    )(gq, gk, gv, la, gr, norm_g.reshape(1, GLA_HEAD_V), scan, ind)


POST_SEGMENTS = 8


def _post_math(alpha, x_ref, osb_ref, og_ref, mod_ref, wg_ref, wua_ref, wub_ref, wo_ref,
               ln1g_ref, ln1b_ref, wf1_ref, wf2_ref, ln2g_ref, ln2b_ref, y_ref, segment):
    x = x_ref[...]
    sh1, sc1, gt1, sh2, sc2, gt2 = [mod_ref[0, m] for m in range(6)]
    d = x.shape[-1]
    assert sum(FF_CHUNKS) == wf1_ref.shape[1] and POST_SEGMENTS == 3 + len(FF_CHUNKS)

    def seg0():
        h = (_ln_stats(x) * (1.0 + sc1) + sh1).astype(BF16)
        return h, _sigmoid(_dot(h, wg_ref[:, :d]))

    h, gate_a = segment(0, seg0)
    part_a, gate_b = segment(1, lambda: (gate_a * _dot(osb_ref[...].astype(BF16), wua_ref[...]),
                                         _sigmoid(_dot(h, wg_ref[:, d:]))))

    def seg2():
        merged = (part_a + gate_b * _dot(og_ref[...].astype(BF16), wub_ref[...])).astype(BF16)
        x1 = _ln_stats(alpha * x + gt1 * _dot(merged, wo_ref[...])) * ln1g_ref[...] + ln1b_ref[...]
        return x1, (_ln_stats(x1) * (1.0 + sc2) + sh2).astype(BF16)

    x1, h2 = segment(2, seg2)
    f = jnp.zeros_like(x)
    lo = 0
    for c, width in enumerate(FF_CHUNKS):
        cols = slice(lo, lo + width)
        lo += width

        def ff(f=f, cols=cols, last=(c == len(FF_CHUNKS) - 1)):
            u = jnp.maximum(_dot(h2, wf1_ref[:, cols]), 0.0)
            f_new = f + _dot((u * u).astype(BF16), wf2_ref[cols, :])
            if last:
                y_ref[...] = _ln_stats(alpha * x1 + gt2 * f_new) * ln2g_ref[...] + ln2b_ref[...]
            return f_new

        f = segment(3 + c, ff)


def _post_kernel(alpha, *refs):
    _post_math(alpha, *refs, segment=lambda j, fn: fn())


def _post(x2d, o_sb, o_g, mod, seq_len, alpha, wg, wua, wub, wo, ln1g, ln1b, wf1, wf2, ln2g, ln2b):
    n, d = x2d.shape
    per_token = mod.shape[2] != 1
    tm = n if per_token else min(TOKEN_TILE, seq_len)
    tiles_per_seq = 1 if per_token else seq_len // tm
    row = lambda cols: pl.BlockSpec((tm, cols), lambda i: (i, 0))
    single = lambda a: pl.BlockSpec(a.shape, lambda i: (0,) * a.ndim, pipeline_mode=pl.Buffered(1))
    vec = lambda a: a.reshape(1, d)
    mod_spec = (_const_spec(mod.shape) if per_token else
                pl.BlockSpec((1, 6, 1, d), lambda i: (i // tiles_per_seq, 0, 0, 0)))
    return pl.pallas_call(
        functools.partial(_post_kernel, alpha),
        grid=(n // tm,),
        in_specs=[row(d), row(SB_WIDTH), row(GLA_V_WIDTH), mod_spec,
                  single(wg), single(wua), single(wub), single(wo),
                  _const_spec((1, d)), _const_spec((1, d)),
                  single(wf1), single(wf2),
                  _const_spec((1, d)), _const_spec((1, d))],
        out_specs=row(d),
        out_shape=jax.ShapeDtypeStruct((n, d), F32),
        compiler_params=_cparams("parallel"),
        name="post",
    )(x2d, o_sb, o_g, mod, wg, wua, wub, wo, vec(ln1g), vec(ln1b), wf1, wf2, vec(ln2g), vec(ln2b))


def _lane_scan_matrix():
    j = np.arange(LANES)[:, None]
    s = np.arange(LANES)[None, :]
    later = (j > s).astype(np.float32)
    half = np.concatenate([later, np.ones((LANES, LANES), np.float32)], axis=1)
    return jnp.asarray(np.concatenate([half, half], axis=0), dtype=BF16)


def _paged_logits(k_pages, qb_scr, bias_ref, lb_scr, hl_scr):
    p, nh, hd = PAGED_PAGES, SB_HEADS, SB_HEAD_DIM
    sublane = lax.broadcasted_iota(jnp.int32, (nh, LANES), 0)
    zs = [jnp.zeros((nh, LANES), F32)] * p
    bias = jnp.zeros((nh, LANES), F32)
    for h in range(nh):
        hrows = slice(h * hd, (h + 1) * hd)
        qh = qb_scr[hrows, :]
        bias = jnp.where(sublane == h, bias_ref[h], bias)
        for s in range(p):
            tot = jnp.sum(k_pages[s, hrows, :] * qh, axis=0, keepdims=True)
            zs[s] = jnp.where(sublane == h, tot, zs[s])
    z = jnp.concatenate([zs[s] + bias for s in range(p)], axis=0)
    soft = jnp.log1p(jnp.exp(-jnp.abs(z)))
    log_beta = jnp.minimum(z, 0.0) - soft
    hi, lo = _split2(log_beta - z)
    lb_scr[...] = log_beta
    hl_scr[...] = jnp.concatenate([hi, lo], axis=1)


def _paged_accumulate(v_pages, r1, first, lb_scr, carry_scr, acc_scr):
    p, nh, hd = PAGED_PAGES, SB_HEADS, SB_HEAD_DIM
    carry = jnp.where(first, 0.0, carry_scr[...])
    row_tot = r1[:, LANES:]
    prefix, run = [], carry
    for s in range(p):
        prefix.append(run)
        run = run + row_tot[s * nh:(s + 1) * nh]
    carry_scr[...] = run
    w = jnp.exp(lb_scr[...] + r1[:, :LANES] + jnp.concatenate(prefix, axis=0))
    for h in range(nh):
        hrows = slice(h * hd, (h + 1) * hd)
        acc = jnp.where(first, 0.0, acc_scr[hrows, :])
        for s in range(p):
            acc = acc + v_pages[s, hrows, :] * w[s * nh + h:s * nh + h + 1, :]
        acc_scr[hrows, :] = acc


def _post_paged_kernel(alpha, n_pages, pt_ref, bias_ref, *refs):
    post_refs = refs[:14]
    q_ref, lscan_ref, kc_hbm, vc_hbm = refs[14:18]
    y_ref, os_ref = refs[18:20]
    kbuf, vbuf, ksem, vsem, carry_scr, acc_scr, qb_scr, lb_scr, hl_scr = refs[20:]
    p, subs, slots = PAGED_PAGES, POST_SEGMENTS, PAGE_SLOTS
    groups = n_pages // p
    i = pl.program_id(0)
    n_steps = pl.num_programs(0)

    def slot_of(t):
        return lax.rem(t + slots, slots)

    def copies(hbm, buf, sem, t):
        b, g, slot = lax.div(t, groups), lax.rem(t, groups), slot_of(t)
        return [pltpu.make_async_copy(hbm.at[pt_ref[b, n_pages - 1 - (g * p + s)]],
                                      buf.at[slot, s], sem.at[slot]) for s in range(p)]

    def start(cs):
        for c in cs:
            c.start()

    def wait(cs):
        for c in cs:
            c.wait()

    def finish_sequence(b):
        acc_t = acc_scr[...].T
        os_ref[pl.ds(b, 1), :] = jnp.sum(acc_t, axis=0, keepdims=True)

    scanned = {}

    def before(j):
        t = subs * i + j
        if j == 0:
            @pl.when(i == 0)
            def _():
                hl_scr[...] = jnp.zeros_like(hl_scr)
                lb_scr[...] = jnp.zeros_like(lb_scr)
                carry_scr[...] = jnp.zeros_like(carry_scr)
                acc_scr[...] = jnp.zeros_like(acc_scr)
                vbuf[slots - 1] = jnp.zeros(vbuf.shape[1:], vbuf.dtype)
                start(copies(kc_hbm, kbuf, ksem, t))
                start(copies(kc_hbm, kbuf, ksem, t + 1))
                start(copies(vc_hbm, vbuf, vsem, t))

            @pl.when(i > 0)
            def _():
                wait(copies(vc_hbm, vbuf, vsem, t - 1))
        else:
            wait(copies(vc_hbm, vbuf, vsem, t - 1))
        wait(copies(kc_hbm, kbuf, ksem, t))
        for ahead, hbm, buf, sem in ((2, kc_hbm, kbuf, ksem), (1, vc_hbm, vbuf, vsem)):
            if j + ahead < subs:
                start(copies(hbm, buf, sem, t + ahead))
            else:
                @pl.when(i + 1 < n_steps)
                def _(ahead=ahead, hbm=hbm, buf=buf, sem=sem):
                    start(copies(hbm, buf, sem, t + ahead))
        scanned[j] = _dot(hl_scr[...], lscan_ref[...])

    def after(j):
        t = subs * i + j
        first = lax.rem(t - 1 + groups, groups) == 0
        _paged_accumulate(vbuf.at[slot_of(t - 1)], scanned[j], first, lb_scr, carry_scr, acc_scr)
        if j == 0:
            @pl.when(jnp.logical_and(lax.rem(t, groups) == 0, t > 0))
            def _():
                finish_sequence(lax.div(t, groups) - 1)

            @pl.when(lax.rem(t, groups) == 0)
            def _():
                q_row = q_ref[pl.ds(lax.div(t, groups), 1), :]
                qb_scr[...] = jnp.broadcast_to(q_row, (LANES, q_row.shape[1])).T
        _paged_logits(kbuf.at[slot_of(t)], qb_scr, bias_ref, lb_scr, hl_scr)

    def segment(j, fn):
        before(j)
        out = fn()
        after(j)
        return out

    _post_math(alpha, *post_refs, y_ref, segment=segment)

    @pl.when(i == n_steps - 1)
    def _():
        t_last = subs * n_steps - 1
        wait(copies(vc_hbm, vbuf, vsem, t_last))
        r1 = _dot(hl_scr[...], lscan_ref[...])
        _paged_accumulate(vbuf.at[slot_of(t_last)], r1, False, lb_scr, carry_scr, acc_scr)
        finish_sequence(lax.div(t_last, groups))


def _post_with_paged_attention(x2d, o_sb, o_g, mod, seq_len, alpha, post_w,
                               page_table, bias, q_s, cache_kT, cache_vT):
    wg, wua, wub, wo, ln1g, ln1b, wf1, wf2, ln2g, ln2b = post_w
    n, d = x2d.shape
    nseq, n_pages = page_table.shape
    p, width = PAGED_PAGES, SB_WIDTH
    tm = min(TOKEN_TILE, seq_len)
    tiles_per_seq = seq_len // tm
    steps = n // tm
    assert cache_kT.shape[1:] == (width, LANES) and n_pages % p == 0
    assert (n_pages // p) % POST_SEGMENTS == 0 and steps * POST_SEGMENTS * p == nseq * n_pages
    lscan = _lane_scan_matrix()

    row = lambda cols: pl.BlockSpec((tm, cols), lambda i, pt: (i, 0))
    const = lambda shape: pl.BlockSpec(shape, lambda i, pt: (0,) * len(shape))
    single = lambda a: pl.BlockSpec(a.shape, lambda i, pt: (0,) * a.ndim,
                                    pipeline_mode=pl.Buffered(1))
    vec = lambda a: a.reshape(1, d)
    hbm = pl.BlockSpec(memory_space=pl.ANY)
    grid_spec = pltpu.PrefetchScalarGridSpec(
        num_scalar_prefetch=1,
        grid=(steps,),
        in_specs=[pl.BlockSpec(memory_space=pltpu.SMEM),
                  row(d), row(SB_WIDTH), row(GLA_V_WIDTH),
                  pl.BlockSpec((1, 6, 1, d), lambda i, pt: (i // tiles_per_seq, 0, 0, 0)),
                  single(wg), single(wua), single(wub), single(wo),
                  const((1, d)), const((1, d)), single(wf1), single(wf2),
                  const((1, d)), const((1, d)),
                  const(q_s.shape), const(lscan.shape), hbm, hbm],
        out_specs=(row(d), const((nseq, width))),
        scratch_shapes=[pltpu.VMEM((PAGE_SLOTS, p, width, LANES), F32),
                        pltpu.VMEM((PAGE_SLOTS, p, width, LANES), F32),
                        pltpu.SemaphoreType.DMA((PAGE_SLOTS,)),
                        pltpu.SemaphoreType.DMA((PAGE_SLOTS,)),
                        pltpu.VMEM((SB_HEADS, LANES), F32),
                        pltpu.VMEM((width, LANES), F32),
                        pltpu.VMEM((width, LANES), F32),
                        pltpu.VMEM((p * SB_HEADS, LANES), F32),
                        pltpu.VMEM((p * SB_HEADS, 2 * LANES), BF16)],
    )
    return pl.pallas_call(
        functools.partial(_post_paged_kernel, alpha, n_pages),
        grid_spec=grid_spec,
        out_shape=(jax.ShapeDtypeStruct((n, d), F32), jax.ShapeDtypeStruct((nseq, width), F32)),
        compiler_params=_cparams("arbitrary"),
        name="post_paged",
    )(page_table, bias, x2d, o_sb, o_g, mod, wg, wua, wub, wo, vec(ln1g), vec(ln1b), wf1, wf2,
      vec(ln2g), vec(ln2b), q_s, lscan, cache_kT, cache_vT)


def _gla_sample_kernel(gq_ref, gk_ref, la_ref, gv_ref, gr_ref, ng_ref, s0_ref, og_ref, st_ref):
    nh, dk, dv = GLA_HEADS, GLA_HEAD_K, GLA_HEAD_V
    la = la_ref[0]
    q_dec = gq_ref[0] * (dk ** -0.5) * jnp.exp(la)
    k_inv = gk_ref[0] * jnp.exp(-la)
    k_end = gk_ref[0]
    s0 = s0_ref[0]
    v8 = gv_ref[0]
    qk = q_dec * k_inv
    qs = q_dec * s0
    outs = []
    for h in range(nh):
        rows = slice(h * dk, (h + 1) * dk)
        att = jnp.sum(qk[rows], axis=0, keepdims=True)
        outs.append(att * v8[h:h + 1] + jnp.sum(qs[rows], axis=0, keepdims=True))
    o = jnp.concatenate(outs + [jnp.zeros((8 - nh, dv), F32)], axis=0)
    o = o * lax.rsqrt(jnp.mean(o * o, axis=-1, keepdims=True) + GN_EPS) * ng_ref[...]
    r = gr_ref[0]
    og_ref[0] = o * (r * _sigmoid(r))
    v_rows = jnp.concatenate(
        [jnp.broadcast_to(v8[h:h + 1], (dk, dv)) for h in range(nh)], axis=0)
    st_ref[0] = jnp.exp(la) * s0 + k_end * v_rows


def _gla_sample(gq, gk, gv, la, gr, norm_g, s0):
    n = gq.shape[0]
    nh, dk, dv = GLA_HEADS, GLA_HEAD_K, GLA_HEAD_V
    kw = nh * dk
    pad_heads = lambda a: jnp.pad(a.reshape(n, nh, dv), ((0, 0), (0, 8 - nh), (0, 0)))
    colspec = pl.BlockSpec((1, kw, 1), lambda b: (b, 0, 0))
    headspec = pl.BlockSpec((1, 8, dv), lambda b: (b, 0, 0))
    statespec = pl.BlockSpec((1, kw, dv), lambda b: (b, 0, 0))
    og, st = pl.pallas_call(
        _gla_sample_kernel,
        grid=(n,),
        in_specs=[colspec, colspec, colspec, headspec, headspec, _const_spec((1, dv)), statespec],
        out_specs=(headspec, statespec),
        out_shape=(jax.ShapeDtypeStruct((n, 8, dv), F32),
                   jax.ShapeDtypeStruct((n, kw, dv), F32)),
        compiler_params=_cparams("parallel"),
        name="gla_sample",
    )(gq.reshape(n, kw, 1), gk.reshape(n, kw, 1), la.reshape(n, kw, 1),
      pad_heads(gv), pad_heads(gr), norm_g.reshape(1, dv), s0)
    return og[:, :nh].reshape(n, nh * dv), st


def kernel(x_prompt, x_sample, cache_k, cache_v, state_gla, page_table, c_prompt, c_sample,
           w_ada, b_ada, w_in, sb_bias, w_gla_g2, b_gla_g, gla_norm_g, w_up_a, w_up_b, w_o,
           ln1_g, ln1_b, w_ff1, w_ff2, ln2_g, ln2_b):
    depth = w_ada.shape[0]
    pb, seq_len, d = x_prompt.shape
    sb_n = x_sample.shape[0]
    assert x_sample.shape[1] == 1
    alpha = (2.0 * depth) ** 0.25
    n_phys, page_size = cache_k.shape[1], cache_k.shape[2]

    yp = x_prompt.reshape(pb * seq_len, d)
    ys = x_sample.reshape(sb_n, d)
    c_all = jnp.concatenate([c_prompt, c_sample], axis=0)
    main_w = 3 * SB_WIDTH + 2 * GLA_K_WIDTH + 2 * GLA_V_WIDTH
    gate_off = main_w + GLA_GATE_RANK

    outs = [[] for _ in range(6)]
    for l in range(depth):
        ada = _ada(c_all, w_ada[l].astype(BF16), b_ada[l])
        mod_p = ada[:pb].reshape(pb, 6, 1, d)
        mod_s = ada[pb:].reshape(sb_n, 6, d).transpose(1, 0, 2).reshape(1, 6, sb_n, d)

        w_in_l = w_in[l]
        wm = w_in_l[:, :main_w].astype(BF16)
        wl = jnp.pad(w_in_l[:, main_w:gate_off], ((0, 0), (0, LANES - GLA_GATE_RANK))).astype(BF16)
        wg = w_in_l[:, gate_off:].astype(BF16)
        wg2 = jnp.pad(w_gla_g2[l], ((0, LANES - GLA_GATE_RANK), (0, 0))).astype(BF16)
        bg = b_gla_g[l].reshape(1, GLA_K_WIDTH)
        post_w = (wg, w_up_a[l].astype(BF16), w_up_b[l].astype(BF16), w_o[l].astype(BF16),
                  ln1_g[l], ln1_b[l], w_ff1[l].astype(BF16), w_ff2[l].astype(BF16),
                  ln2_g[l], ln2_b[l])

        qT3, k_p, v_p, k_bf, vT3, gq, gk, gv, gr, la = _proj_prompt(yp, mod_p, seq_len, wm, wl, wg2, bg)
        q_s, k_s, v_s, gq_s, gk_s, gv_s, gr_s, la_s = _proj_sample(ys, mod_s, wm, wl, wg2, bg)
        o_sb = _sb_prompt(sb_bias[l], qT3, k_bf, vT3, pb, seq_len)
        o_g, st_p = _gla_prompt(gq, gk, gv, la, gr, gla_norm_g[l], pb, seq_len)

        page_major = lambda c: jnp.transpose(c, (0, 2, 3, 1)).reshape(n_phys, SB_WIDTH, page_size)
        yp, o_sb_s = _post_with_paged_attention(
            yp, o_sb, o_g, mod_p, seq_len, alpha, post_w, page_table, sb_bias[l], q_s,
            page_major(cache_k[l]), page_major(cache_v[l]))

        s0 = state_gla[l].reshape(sb_n, GLA_HEADS * GLA_HEAD_K, GLA_HEAD_V)
        o_g_s, st_s = _gla_sample(gq_s, gk_s, gv_s, la_s, gr_s, gla_norm_g[l], s0)
        ys = _post(ys, o_sb_s, o_g_s, mod_s, 1, alpha, *post_w)

        rows_major = lambda t: jnp.transpose(
            t.reshape(pb, SB_HEADS, SB_HEAD_DIM, seq_len), (0, 3, 1, 2))
        outs[0].append(rows_major(k_p))
        outs[1].append(rows_major(v_p))
        outs[2].append(st_p.reshape(pb, GLA_HEADS, GLA_HEAD_K, GLA_HEAD_V))
        outs[3].append(k_s.reshape(sb_n, 1, SB_HEADS, SB_HEAD_DIM))
        outs[4].append(v_s.reshape(sb_n, 1, SB_HEADS, SB_HEAD_DIM))
        outs[5].append(st_s.reshape(sb_n, GLA_HEADS, GLA_HEAD_K, GLA_HEAD_V))

    stacked = [jnp.stack(o) for o in outs]
    return (yp.reshape(pb, seq_len, d), ys.reshape(sb_n, 1, d),
            stacked[0], stacked[1], stacked[2].astype(state_gla.dtype),
            stacked[3], stacked[4], stacked[5].astype(state_gla.dtype))
```

```python
import functools

import numpy as np
import jax
import jax.numpy as jnp
from jax import lax
from jax.experimental import pallas as pl
from jax.experimental.pallas import tpu as pltpu

F32 = jnp.float32
BF16 = jnp.bfloat16

SB_HEADS = 8
SB_HEAD_DIM = 64
SB_WIDTH = SB_HEADS * SB_HEAD_DIM
GLA_HEADS = 4
GLA_HEAD_K = 64
GLA_HEAD_V = 128
GLA_K_WIDTH = GLA_HEADS * GLA_HEAD_K
GLA_V_WIDTH = GLA_HEADS * GLA_HEAD_V
GLA_GATE_RANK = 16
GLA_TAU = 16.0
GLA_CHUNK = 64
LN_EPS = 1e-5
GN_EPS = 1e-6

LANES = 128
VMEM_LIMIT_BYTES = 56 * 1024 * 1024

TOKEN_TILE = 512
PROJ_TOKEN_TILE = 1024
SB_Q_TILE = 256
SB_K_TILE = 128
SB_CHAIN_SKEW = 10
SB_HEADS_PER_STEP = 8
GLA_BLOCK = 256
GLA_SAMPLE_SEQS = 8
PAGED_PAGES = 8
FF_CHUNKS = (768, 768, 768, 768, 1024)
PAGE_SLOTS = 3


def _cparams(*sem):
    return pltpu.CompilerParams(dimension_semantics=sem, vmem_limit_bytes=VMEM_LIMIT_BYTES)


def _dot(a, b):
    return jnp.dot(a, b, preferred_element_type=F32)


def _dot_nt(a, b):
    return lax.dot_general(a, b, (((1,), (1,)), ((), ())), preferred_element_type=F32)


def _ln_stats(x):
    mu = jnp.mean(x, axis=-1, keepdims=True)
    xc = x - mu
    var = jnp.mean(xc * xc, axis=-1, keepdims=True)
    return xc * lax.rsqrt(var + LN_EPS)


def _log_sigmoid(x):
    return jnp.minimum(x, 0.0) - jnp.log1p(jnp.exp(-jnp.abs(x)))


def _sigmoid(x):
    return 1.0 / (1.0 + jnp.exp(-x))


LOG2E = 1.4426950408889634


def _neg_abs(x):
    bits = lax.bitcast_convert_type(x, jnp.int32) | jnp.int32(-2 ** 31)
    return lax.bitcast_convert_type(bits, F32)


def _div_pow2(x, n):
    assert n & (n - 1) == 0
    return jnp.right_shift(x, n.bit_length() - 1)


def _split2(x):
    hi = x.astype(BF16)
    lo = (x - hi.astype(F32)).astype(BF16)
    return hi, lo


def _split3(x):
    hi = x.astype(BF16)
    r = x - hi.astype(F32)
    mid = r.astype(BF16)
    lo = (r - mid.astype(F32)).astype(BF16)
    return hi, mid, lo


def _ada_kernel(c_ref, w_ref, b_ref, o_ref):
    c = c_ref[...]
    a = c * _sigmoid(c)
    o_ref[...] = _dot(a.astype(BF16), w_ref[...]) + b_ref[...]


def _ada(c_all, w_ada_b, b_ada):
    n, d = c_all.shape
    nout = w_ada_b.shape[1]
    blk = d
    return pl.pallas_call(
        _ada_kernel,
        grid=(nout // blk,),
        in_specs=[pl.BlockSpec((n, d), lambda j: (0, 0)),
                  pl.BlockSpec((d, blk), lambda j: (0, j)),
                  pl.BlockSpec((1, blk), lambda j: (0, j))],
        out_specs=pl.BlockSpec((n, blk), lambda j: (0, j)),
        out_shape=jax.ShapeDtypeStruct((n, nout), F32),
        compiler_params=_cparams("parallel"),
        name="ada",
    )(c_all, w_ada_b, b_ada.reshape(1, nout))


def _proj_math(x, sh1, sc1, wm_ref, wl_ref, wg2_ref, bg_ref):
    h = (_ln_stats(x) * (1.0 + sc1) + sh1).astype(BF16)
    y = _dot(h, wm_ref[...])
    g_low = _dot(h, wl_ref[...])
    gl = _dot(g_low.astype(BF16), wg2_ref[...]) + bg_ref[...]
    log_a = _log_sigmoid(gl) * (1.0 / GLA_TAU)
    return y, log_a


def _proj_prompt_kernel(x_ref, mod_ref, wm_ref, wl_ref, wg2_ref, bg_ref,
                        qT_ref, kT_ref, vT_ref, kb_ref, vTb_ref, gq_ref, gk_ref, gv_ref, gr_ref, la_ref):
    y, log_a = _proj_math(x_ref[...], mod_ref[0, 0], mod_ref[0, 1], wm_ref, wl_ref, wg2_ref, bg_ref)
    w = SB_WIDTH
    q = y[:, 0:w] * (SB_HEAD_DIM ** -0.5 * LOG2E)
    k = y[:, w:2 * w]
    v = y[:, 2 * w:3 * w]
    kb_ref[...] = k.astype(BF16)
    qT = q.T.astype(BF16)
    vT = v.T
    kT_ref[0] = k.T
    vT_ref[0] = vT
    vT = vT.astype(BF16)
    for c in range(qT_ref.shape[0]):
        qT_ref[c] = qT[:, c * SB_Q_TILE:(c + 1) * SB_Q_TILE]
    for c in range(vTb_ref.shape[0]):
        vTb_ref[c] = vT[:, c * SB_K_TILE:(c + 1) * SB_K_TILE]
    o = 3 * w
    gq_ref[...] = y[:, o:o + GLA_K_WIDTH]
    gk_ref[...] = y[:, o + GLA_K_WIDTH:o + 2 * GLA_K_WIDTH]
    o += 2 * GLA_K_WIDTH
    gv_ref[...] = y[:, o:o + GLA_V_WIDTH].astype(BF16)
    gr_ref[...] = y[:, o + GLA_V_WIDTH:o + 2 * GLA_V_WIDTH]
    la_ref[...] = log_a


def _proj_sample_kernel(x_ref, mod_ref, wm_ref, wl_ref, wg2_ref, bg_ref,
                        q_ref, k_ref, v_ref, gq_ref, gk_ref, gv_ref, gr_ref, la_ref):
    y, log_a = _proj_math(x_ref[...], mod_ref[0, 0], mod_ref[0, 1], wm_ref, wl_ref, wg2_ref, bg_ref)
    w = SB_WIDTH
    q_ref[...] = y[:, 0:w] * (SB_HEAD_DIM ** -0.5)
    k_ref[...] = y[:, w:2 * w]
    v_ref[...] = y[:, 2 * w:3 * w]
    o = 3 * w
    gq_ref[...] = y[:, o:o + GLA_K_WIDTH]
    gk_ref[...] = y[:, o + GLA_K_WIDTH:o + 2 * GLA_K_WIDTH]
    o += 2 * GLA_K_WIDTH
    gv_ref[...] = y[:, o:o + GLA_V_WIDTH]
    gr_ref[...] = y[:, o + GLA_V_WIDTH:o + 2 * GLA_V_WIDTH]
    la_ref[...] = log_a


def _const_spec(shape):
    nd = len(shape)
    return pl.BlockSpec(shape, lambda *a: (0,) * nd)


def _proj_prompt(x2d, mod, seq_len, wm, wl, wg2, bg):
    n, d = x2d.shape
    tm = min(PROJ_TOKEN_TILE, seq_len)
    assert seq_len % tm == 0 and tm % SB_Q_TILE == 0
    tiles_per_seq = seq_len // tm
    row = lambda cols: pl.BlockSpec((tm, cols), lambda i: (i, 0))
    dims_major = pl.BlockSpec((1, SB_WIDTH, tm),
                              lambda i: (i // tiles_per_seq, 0, i % tiles_per_seq))
    out_shape = (
        jax.ShapeDtypeStruct((n // SB_Q_TILE, SB_WIDTH, SB_Q_TILE), BF16),
        jax.ShapeDtypeStruct((n // seq_len, SB_WIDTH, seq_len), F32),
        jax.ShapeDtypeStruct((n // seq_len, SB_WIDTH, seq_len), F32),
        jax.ShapeDtypeStruct((n, SB_WIDTH), BF16),
        jax.ShapeDtypeStruct((n // SB_K_TILE, SB_WIDTH, SB_K_TILE), BF16),
        jax.ShapeDtypeStruct((n, GLA_K_WIDTH), F32),
        jax.ShapeDtypeStruct((n, GLA_K_WIDTH), F32),
        jax.ShapeDtypeStruct((n, GLA_V_WIDTH), BF16),
        jax.ShapeDtypeStruct((n, GLA_V_WIDTH), F32),
        jax.ShapeDtypeStruct((n, GLA_K_WIDTH), F32),
    )
    out_specs = (
        pl.BlockSpec((tm // SB_Q_TILE, SB_WIDTH, SB_Q_TILE), lambda i: (i, 0, 0)),
        dims_major, dims_major, row(SB_WIDTH),
        pl.BlockSpec((tm // SB_K_TILE, SB_WIDTH, SB_K_TILE), lambda i: (i, 0, 0)),
        row(GLA_K_WIDTH), row(GLA_K_WIDTH), row(GLA_V_WIDTH), row(GLA_V_WIDTH), row(GLA_K_WIDTH),
    )
    return pl.pallas_call(
        _proj_prompt_kernel,
        grid=(n // tm,),
        in_specs=[row(d),
                  pl.BlockSpec((1, 6, 1, d), lambda i: (i // tiles_per_seq, 0, 0, 0)),
                  _const_spec(wm.shape), _const_spec(wl.shape), _const_spec(wg2.shape),
                  _const_spec(bg.shape)],
        out_specs=out_specs,
        out_shape=out_shape,
        compiler_params=_cparams("parallel"),
        name="proj_prompt",
    )(x2d, mod, wm, wl, wg2, bg)


def _proj_sample(x2d, mod, wm, wl, wg2, bg):
    n, d = x2d.shape
    full = lambda cols: pl.BlockSpec((n, cols), lambda i: (0, 0))
    widths = (SB_WIDTH, SB_WIDTH, SB_WIDTH, GLA_K_WIDTH, GLA_K_WIDTH, GLA_V_WIDTH, GLA_V_WIDTH,
              GLA_K_WIDTH)
    return pl.pallas_call(
        _proj_sample_kernel,
        grid=(1,),
        in_specs=[full(d), _const_spec(mod.shape),
                  _const_spec(wm.shape), _const_spec(wl.shape), _const_spec(wg2.shape),
                  _const_spec(bg.shape)],
        out_specs=tuple(full(c) for c in widths),
        out_shape=tuple(jax.ShapeDtypeStruct((n, c), F32) for c in widths),
        compiler_params=_cparams("arbitrary"),
        name="proj_sample",
    )(x2d, mod, wm, wl, wg2, bg)


def _sb_scan_matrix():
    k = SB_K_TILE
    s = np.arange(k)[:, None]
    j = np.arange(k)[None, :]
    upper = (j >= s).astype(np.float32)
    top = np.concatenate([upper, upper], axis=1)
    ones = np.ones((16, 2 * k), np.float32)
    return jnp.asarray(np.concatenate([top, ones], axis=0), dtype=BF16)


def _sb_prompt_kernel(bias_ref, qT_ref, k_ref, vT_ref, a_ref, o_ref):
    hg = pl.program_id(1)
    i = pl.program_id(2)
    tq, tk, hd, nh = SB_Q_TILE, SB_K_TILE, SB_HEAD_DIM, SB_HEADS_PER_STEP
    n_diag = tq // tk
    assert n_diag == 2
    scan = a_ref[...]
    head_row = lax.broadcasted_iota(jnp.int32, (2 * hd, tq), 0)
    key_in_blk = lax.broadcasted_iota(jnp.int32, (tk, tq), 0)
    q_in_tile = lax.broadcasted_iota(jnp.int32, (tk, tq), 1)
    diag_valid = [(key_in_blk + (n_diag - 1 - u) * tk) < q_in_tile for u in range(n_diag)]

    bias_row = lax.broadcasted_iota(jnp.int32, (2 * hd, tq), 0)
    ones_cols = jnp.where(lax.broadcasted_iota(jnp.int32, (tk, 2 * hd), 1) < 3, 1.0, 0.0
                          ).astype(BF16)
    qms = []
    for h in range(nh):
        pair = qT_ref[0, (h // 2) * 2 * hd:(h // 2 + 1) * 2 * hd, :].astype(F32)
        own = (head_row < hd) if h % 2 == 0 else (head_row >= hd)
        parts = _split3(jnp.full((2 * hd, tq), bias_ref[nh * hg + h] * LOG2E, F32))
        bias_tile = jnp.zeros((2 * hd, tq), F32)
        for r, part in enumerate(parts):
            bias_tile = jnp.where(bias_row == r, part.astype(F32), bias_tile)
        qms.append(jnp.concatenate([jnp.where(own, pair, 0.0), bias_tile], axis=0).astype(BF16))

    def logits(kb, h):
        lanes = slice((h // 2) * 2 * hd, (h // 2 + 1) * 2 * hd)
        kblk = k_ref[pl.ds(pl.multiple_of(kb * tk, tk), tk), lanes]
        return _dot(jnp.concatenate([kblk, ones_cols], axis=1), qms[h])

    def key_group(first_kb, state, masked):
        carries, accs = list(state[:nh]), list(state[nh:])
        order = [(u, h) for u in range(n_diag) for h in range(nh)]
        zs, mids, rs, ws = {}, {}, {}, {}

        def stage_logits(c):
            u, h = order[c]
            zs[c] = logits(first_kb + (n_diag - 1 - u), h)

        def stage_scan(c):
            u, h = order[c]
            cols = slice(tq // 2, tq) if (masked and u == 0) else slice(0, tq)
            z = zs.pop(c)[:, cols]
            drop = jnp.maximum(z, 0.0) + jnp.log(1.0 + jnp.exp2(_neg_abs(z))) * LOG2E
            valid = None
            if masked:
                valid = diag_valid[u][:, cols]
                drop = jnp.where(valid, drop, 0.0)
            hi, lo = _split2(drop)
            mids[c] = (z, valid, cols)
            rs[c] = _dot(scan, jnp.concatenate([hi, lo], axis=0))

        def stage_values(c):
            u, h = order[c]
            z, valid, cols = mids.pop(c)
            r = rs.pop(c)
            n_cols = cols.stop - cols.start
            w = jnp.exp2((z - r[:tk]).reshape(tk // 8, 8, n_cols) - carries[h][:, cols]
                         ).reshape(tk, n_cols)
            if masked:
                w = jnp.where(valid, w, 0.0)
            total = r[tk:tk + 8]
            if n_cols < tq:
                w = jnp.concatenate([jnp.zeros((tk, tq - n_cols), F32), w], axis=1)
                total = jnp.concatenate([jnp.zeros((8, tq - n_cols), F32), total], axis=1)
            ws.setdefault(h, []).append(w.astype(BF16))
            carries[h] = carries[h] + total
            if u == n_diag - 1:
                v_cat = jnp.concatenate(
                    [vT_ref[first_kb + (n_diag - 1 - uu), h * hd:(h + 1) * hd, :]
                     for uu in range(n_diag)], axis=1)
                accs[h] = accs[h] + _dot(v_cat, jnp.concatenate(ws.pop(h), axis=0))

        n_chains = len(order)
        for s in range(n_chains + 2 * SB_CHAIN_SKEW):
            if s < n_chains:
                stage_logits(s)
            if 0 <= s - SB_CHAIN_SKEW < n_chains:
                stage_scan(s - SB_CHAIN_SKEW)
            if 0 <= s - 2 * SB_CHAIN_SKEW < n_chains:
                stage_values(s - 2 * SB_CHAIN_SKEW)
        return tuple(carries) + tuple(accs)

    state = tuple([jnp.zeros((8, tq), F32)] * nh + [jnp.zeros((hd, tq), F32)] * nh)
    state = key_group(n_diag * i, state, True)
    state = lax.fori_loop(0, i, lambda t, st: key_group(n_diag * (i - 1 - t), st, False), state)

    oT = jnp.concatenate(state[nh:], axis=0)
    o_ref[...] = oT.T.astype(o_ref.dtype)


def _sb_prompt(bias, qT3, k_bf, vT3, batch, seq_len):
    n = k_bf.shape[0]
    tq, tk = SB_Q_TILE, SB_K_TILE
    assert seq_len % tq == 0
    qt = seq_len // tq
    kblocks = seq_len // tk
    width = SB_HEADS_PER_STEP * SB_HEAD_DIM
    scan = _sb_scan_matrix()
    return pl.pallas_call(
        _sb_prompt_kernel,
        grid=(batch, SB_HEADS // SB_HEADS_PER_STEP, qt),
        in_specs=[pl.BlockSpec(memory_space=pltpu.SMEM),
                  pl.BlockSpec((1, width, tq), lambda b, hg, i: (b * qt + i, hg, 0)),
                  pl.BlockSpec((seq_len, width), lambda b, hg, i: (b, hg)),
                  pl.BlockSpec((kblocks, width, tk), lambda b, hg, i: (b, hg, 0)),
                  _const_spec(scan.shape)],
        out_specs=pl.BlockSpec((tq, width), lambda b, hg, i: (b * qt + i, hg)),
        out_shape=jax.ShapeDtypeStruct((n, SB_WIDTH), BF16),
        compiler_params=_cparams("parallel", "parallel", "arbitrary"),
        name="sb_prompt",
    )(bias, qT3, k_bf, vT3, scan)


def _gla_consts():
    t = np.arange(GLA_BLOCK)
    same = (t[:, None] // GLA_CHUNK) == (t[None, :] // GLA_CHUNK)
    tri = (same & (t[None, :] <= t[:, None])).astype(np.float32)
    ones = same.astype(np.float32)
    scan = np.concatenate([np.concatenate([tri] * 3, axis=1),
                           np.concatenate([ones] * 3, axis=1)], axis=0)
    n_chunks = GLA_BLOCK // GLA_CHUNK
    ind = np.zeros((GLA_BLOCK, n_chunks * LANES), np.float32)
    for c in range(n_chunks):
        ind[c * GLA_CHUNK:(c + 1) * GLA_CHUNK, c * LANES:(c + 1) * LANES] = 1.0
    ind = np.concatenate([ind] * 3, axis=0)
    return jnp.asarray(scan, dtype=BF16), jnp.asarray(ind, dtype=BF16)


def _gla_prompt_kernel(gq_ref, gk_ref, gv_ref, la_ref, gr_ref, ng_ref, scan_ref, ind_ref,
                       og_ref, st_ref, s_scr):
    j = pl.program_id(1)
    tb, c_len = GLA_BLOCK, GLA_CHUNK
    n_chunks = tb // c_len
    dk, dv, nh = GLA_HEAD_K, GLA_HEAD_V, GLA_HEADS

    @pl.when(j == 0)
    def _():
        s_scr[...] = jnp.zeros_like(s_scr)

    la = la_ref[...]
    parts = jnp.concatenate(_split3(la), axis=0)
    bb = _dot(scan_ref[...], parts)
    b = bb[:tb]
    b_last = bb[tb:]
    laT_parts = jnp.concatenate(_split3(la.T), axis=1)
    b_last_col = _dot(laT_parts, ind_ref[...])

    q = gq_ref[...] * (dk ** -0.5)
    k = gk_ref[...]
    v = gv_ref[...]
    q_dec = q * jnp.exp(b)
    k_inv = (k * jnp.exp(-b)).astype(BF16)
    k_endT = (k * jnp.exp(b_last - b)).T

    lane_head = _div_pow2(lax.broadcasted_iota(jnp.int32, (tb, nh * dk), 1), dk)
    ti = lax.broadcasted_iota(jnp.int32, (tb, tb), 0)
    tj = lax.broadcasted_iota(jnp.int32, (tb, tb), 1)
    tj_chunk = _div_pow2(tj, c_len)
    causal = (_div_pow2(ti, c_len) == tj_chunk) & (tj <= ti)

    qm = [jnp.where(lane_head == h, q_dec, 0.0).astype(BF16) for h in range(nh)]
    scores = [_dot_nt(qm[h], k_inv) for h in range(nh)]
    us = []
    for c in range(n_chunks):
        k_endT_c = jnp.where(tj_chunk == c, k_endT, 0.0).astype(BF16)
        us.append(_dot(k_endT_c, v))
    o_intra = [_dot(jnp.where(causal, scores[h], 0.0).astype(BF16), v[:, h * dv:(h + 1) * dv])
               for h in range(nh)]

    states = [s_scr[...]]
    for c in range(n_chunks):
        u_diag = jnp.concatenate(
            [us[c][h * dk:(h + 1) * dk, h * dv:(h + 1) * dv] for h in range(nh)], axis=0)
        states.append(jnp.exp(b_last_col[:, c * LANES:(c + 1) * LANES]) * states[c] + u_diag)
    o_inter = []
    for c in range(n_chunks):
        rows = slice(c * c_len, (c + 1) * c_len)
        q_stack = jnp.concatenate([qm[h][rows] for h in range(nh)], axis=0)
        o_inter.append(_dot(q_stack, states[c].astype(BF16)))

    norm_g = ng_ref[...]
    for c in range(n_chunks):
        rows = slice(c * c_len, (c + 1) * c_len)
        for h in range(nh):
            o = o_intra[h][rows] + o_inter[c][h * c_len:(h + 1) * c_len]
            o = o * lax.rsqrt(jnp.mean(o * o, axis=-1, keepdims=True) + GN_EPS) * norm_g
            r = gr_ref[rows, h * dv:(h + 1) * dv]
            og_ref[rows, h * dv:(h + 1) * dv] = (o * (r * _sigmoid(r))).astype(og_ref.dtype)
    s_all = states[n_chunks]
    s_scr[...] = s_all

    @pl.when(j == pl.num_programs(1) - 1)
    def _():
        st_ref[0] = s_all


def _gla_prompt(gq, gk, gv, la, gr, norm_g, batch, seq_len):
    n = gq.shape[0]
    tb = GLA_BLOCK
    assert seq_len % tb == 0
    nb = seq_len // tb
    scan, ind = _gla_consts()
    row = lambda cols: pl.BlockSpec((tb, cols), lambda b, j: (b * nb + j, 0))
    rows_s = GLA_HEADS * GLA_HEAD_K
    return pl.pallas_call(
        _gla_prompt_kernel,
        grid=(batch, nb),
        in_specs=[row(GLA_K_WIDTH), row(GLA_K_WIDTH), row(GLA_V_WIDTH), row(GLA_K_WIDTH),
                  row(GLA_V_WIDTH), _const_spec((1, GLA_HEAD_V)),
                  _const_spec(scan.shape), _const_spec(ind.shape)],
        out_specs=(row(GLA_V_WIDTH),
                   pl.BlockSpec((1, rows_s, GLA_HEAD_V), lambda b, j: (b, 0, 0))),
        out_shape=(jax.ShapeDtypeStruct((n, GLA_V_WIDTH), BF16),
                   jax.ShapeDtypeStruct((batch, rows_s, GLA_HEAD_V), F32)),
        scratch_shapes=[pltpu.VMEM((rows_s, GLA_HEAD_V), F32)],
        compiler_params=_cparams("parallel", "arbitrary"),
        name="gla_prompt",
    )(gq, gk, gv, la, gr, norm_g.reshape(1, GLA_HEAD_V), scan, ind)


POST_SEGMENTS = 8


def _post_math(alpha, x_ref, osb_ref, og_ref, mod_ref, wg_ref, wua_ref, wub_ref, wo_ref,
               ln1g_ref, ln1b_ref, wf1_ref, wf2_ref, ln2g_ref, ln2b_ref, y_ref, segment):
    x = x_ref[...]
    sh1, sc1, gt1, sh2, sc2, gt2 = [mod_ref[0, m] for m in range(6)]
    d = x.shape[-1]
    assert sum(FF_CHUNKS) == wf1_ref.shape[1] and POST_SEGMENTS == 3 + len(FF_CHUNKS)

    def seg0():
        h = (_ln_stats(x) * (1.0 + sc1) + sh1).astype(BF16)
        return h, _sigmoid(_dot(h, wg_ref[:, :d]))

    h, gate_a = segment(0, seg0)
    part_a, gate_b = segment(1, lambda: (gate_a * _dot(osb_ref[...].astype(BF16), wua_ref[...]),
                                         _sigmoid(_dot(h, wg_ref[:, d:]))))

    def seg2():
        merged = (part_a + gate_b * _dot(og_ref[...].astype(BF16), wub_ref[...])).astype(BF16)
        x1 = _ln_stats(alpha * x + gt1 * _dot(merged, wo_ref[...])) * ln1g_ref[...] + ln1b_ref[...]
        return x1, (_ln_stats(x1) * (1.0 + sc2) + sh2).astype(BF16)

    x1, h2 = segment(2, seg2)
    f = jnp.zeros_like(x)
    lo = 0
    for c, width in enumerate(FF_CHUNKS):
        cols = slice(lo, lo + width)
        lo += width

        def ff(f=f, cols=cols, last=(c == len(FF_CHUNKS) - 1)):
            u = jnp.maximum(_dot(h2, wf1_ref[:, cols]), 0.0)
            f_new = f + _dot((u * u).astype(BF16), wf2_ref[cols, :])
            if last:
                y_ref[...] = _ln_stats(alpha * x1 + gt2 * f_new) * ln2g_ref[...] + ln2b_ref[...]
            return f_new

        f = segment(3 + c, ff)


def _post_kernel(alpha, *refs):
    _post_math(alpha, *refs, segment=lambda j, fn: fn())


def _post(x2d, o_sb, o_g, mod, seq_len, alpha, wg, wua, wub, wo, ln1g, ln1b, wf1, wf2, ln2g, ln2b):
    n, d = x2d.shape
    per_token = mod.shape[2] != 1
    tm = n if per_token else min(TOKEN_TILE, seq_len)
    tiles_per_seq = 1 if per_token else seq_len // tm
    row = lambda cols: pl.BlockSpec((tm, cols), lambda i: (i, 0))
    single = lambda a: pl.BlockSpec(a.shape, lambda i: (0,) * a.ndim, pipeline_mode=pl.Buffered(1))
    vec = lambda a: a.reshape(1, d)
    mod_spec = (_const_spec(mod.shape) if per_token else
                pl.BlockSpec((1, 6, 1, d), lambda i: (i // tiles_per_seq, 0, 0, 0)))
    return pl.pallas_call(
        functools.partial(_post_kernel, alpha),
        grid=(n // tm,),
        in_specs=[row(d), row(SB_WIDTH), row(GLA_V_WIDTH), mod_spec,
                  single(wg), single(wua), single(wub), single(wo),
                  _const_spec((1, d)), _const_spec((1, d)),
                  single(wf1), single(wf2),
                  _const_spec((1, d)), _const_spec((1, d))],
        out_specs=row(d),
        out_shape=jax.ShapeDtypeStruct((n, d), F32),
        compiler_params=_cparams("parallel"),
        name="post",
    )(x2d, o_sb, o_g, mod, wg, wua, wub, wo, vec(ln1g), vec(ln1b), wf1, wf2, vec(ln2g), vec(ln2b))


def _lane_scan_matrix():
    j = np.arange(LANES)[:, None]
    s = np.arange(LANES)[None, :]
    later = (j > s).astype(np.float32)
    half = np.concatenate([later, np.ones((LANES, LANES), np.float32)], axis=1)
    return jnp.asarray(np.concatenate([half, half], axis=0), dtype=BF16)


def _paged_logits(k_pages, qb_scr, bias_ref, lb_scr, hl_scr):
    p, nh, hd = PAGED_PAGES, SB_HEADS, SB_HEAD_DIM
    sublane = lax.broadcasted_iota(jnp.int32, (nh, LANES), 0)
    zs = [jnp.zeros((nh, LANES), F32)] * p
    bias = jnp.zeros((nh, LANES), F32)
    for h in range(nh):
        hrows = slice(h * hd, (h + 1) * hd)
        qh = qb_scr[hrows, :]
        bias = jnp.where(sublane == h, bias_ref[h], bias)
        for s in range(p):
            tot = jnp.sum(k_pages[s, hrows, :] * qh, axis=0, keepdims=True)
            zs[s] = jnp.where(sublane == h, tot, zs[s])
    z = jnp.concatenate([zs[s] + bias for s in range(p)], axis=0)
    soft = jnp.log1p(jnp.exp(-jnp.abs(z)))
    log_beta = jnp.minimum(z, 0.0) - soft
    hi, lo = _split2(log_beta - z)
    lb_scr[...] = log_beta
    hl_scr[...] = jnp.concatenate([hi, lo], axis=1)


def _paged_accumulate(v_pages, r1, first, lb_scr, carry_scr, acc_scr):
    p, nh, hd = PAGED_PAGES, SB_HEADS, SB_HEAD_DIM
    carry = jnp.where(first, 0.0, carry_scr[...])
    row_tot = r1[:, LANES:]
    prefix, run = [], carry
    for s in range(p):
        prefix.append(run)
        run = run + row_tot[s * nh:(s + 1) * nh]
    carry_scr[...] = run
    w = jnp.exp(lb_scr[...] + r1[:, :LANES] + jnp.concatenate(prefix, axis=0))
    for h in range(nh):
        hrows = slice(h * hd, (h + 1) * hd)
        acc = jnp.where(first, 0.0, acc_scr[hrows, :])
        for s in range(p):
            acc = acc + v_pages[s, hrows, :] * w[s * nh + h:s * nh + h + 1, :]
        acc_scr[hrows, :] = acc


def _post_paged_kernel(alpha, n_pages, pt_ref, bias_ref, *refs):
    post_refs = refs[:14]
    q_ref, lscan_ref, kc_hbm, vc_hbm = refs[14:18]
    y_ref, os_ref = refs[18:20]
    kbuf, vbuf, ksem, vsem, carry_scr, acc_scr, qb_scr, lb_scr, hl_scr = refs[20:]
    p, subs, slots = PAGED_PAGES, POST_SEGMENTS, PAGE_SLOTS
    groups = n_pages // p
    i = pl.program_id(0)
    n_steps = pl.num_programs(0)

    def slot_of(t):
        return lax.rem(t + slots, slots)

    def copies(hbm, buf, sem, t):
        b, g, slot = lax.div(t, groups), lax.rem(t, groups), slot_of(t)
        return [pltpu.make_async_copy(hbm.at[pt_ref[b, n_pages - 1 - (g * p + s)]],
                                      buf.at[slot, s], sem.at[slot]) for s in range(p)]

    def start(cs):
        for c in cs:
            c.start()

    def wait(cs):
        for c in cs:
            c.wait()

    def finish_sequence(b):
        acc_t = acc_scr[...].T
        os_ref[pl.ds(b, 1), :] = jnp.sum(acc_t, axis=0, keepdims=True)

    scanned = {}

    def before(j):
        t = subs * i + j
        if j == 0:
            @pl.when(i == 0)
            def _():
                hl_scr[...] = jnp.zeros_like(hl_scr)
                lb_scr[...] = jnp.zeros_like(lb_scr)
                carry_scr[...] = jnp.zeros_like(carry_scr)
                acc_scr[...] = jnp.zeros_like(acc_scr)
                vbuf[slots - 1] = jnp.zeros(vbuf.shape[1:], vbuf.dtype)
                start(copies(kc_hbm, kbuf, ksem, t))
                start(copies(kc_hbm, kbuf, ksem, t + 1))
                start(copies(vc_hbm, vbuf, vsem, t))

            @pl.when(i > 0)
            def _():
                wait(copies(vc_hbm, vbuf, vsem, t - 1))
        else:
            wait(copies(vc_hbm, vbuf, vsem, t - 1))
        wait(copies(kc_hbm, kbuf, ksem, t))
        for ahead, hbm, buf, sem in ((2, kc_hbm, kbuf, ksem), (1, vc_hbm, vbuf, vsem)):
            if j + ahead < subs:
                start(copies(hbm, buf, sem, t + ahead))
            else:
                @pl.when(i + 1 < n_steps)
                def _(ahead=ahead, hbm=hbm, buf=buf, sem=sem):
                    start(copies(hbm, buf, sem, t + ahead))
        scanned[j] = _dot(hl_scr[...], lscan_ref[...])

    def after(j):
        t = subs * i + j
        first = lax.rem(t - 1 + groups, groups) == 0
        _paged_accumulate(vbuf.at[slot_of(t - 1)], scanned[j], first, lb_scr, carry_scr, acc_scr)
        if j == 0:
            @pl.when(jnp.logical_and(lax.rem(t, groups) == 0, t > 0))
            def _():
                finish_sequence(lax.div(t, groups) - 1)

            @pl.when(lax.rem(t, groups) == 0)
            def _():
                q_row = q_ref[pl.ds(lax.div(t, groups), 1), :]
                qb_scr[...] = jnp.broadcast_to(q_row, (LANES, q_row.shape[1])).T
        _paged_logits(kbuf.at[slot_of(t)], qb_scr, bias_ref, lb_scr, hl_scr)

    def segment(j, fn):
        before(j)
        out = fn()
        after(j)
        return out

    _post_math(alpha, *post_refs, y_ref, segment=segment)

    @pl.when(i == n_steps - 1)
    def _():
        t_last = subs * n_steps - 1
        wait(copies(vc_hbm, vbuf, vsem, t_last))
        r1 = _dot(hl_scr[...], lscan_ref[...])
        _paged_accumulate(vbuf.at[slot_of(t_last)], r1, False, lb_scr, carry_scr, acc_scr)
        finish_sequence(lax.div(t_last, groups))


def _post_with_paged_attention(x2d, o_sb, o_g, mod, seq_len, alpha, post_w,
                               page_table, bias, q_s, cache_kT, cache_vT):
    wg, wua, wub, wo, ln1g, ln1b, wf1, wf2, ln2g, ln2b = post_w
    n, d = x2d.shape
    nseq, n_pages = page_table.shape
    p, width = PAGED_PAGES, SB_WIDTH
    tm = min(TOKEN_TILE, seq_len)
    tiles_per_seq = seq_len // tm
    steps = n // tm
    assert cache_kT.shape[1:] == (width, LANES) and n_pages % p == 0
    assert (n_pages // p) % POST_SEGMENTS == 0 and steps * POST_SEGMENTS * p == nseq * n_pages
    lscan = _lane_scan_matrix()

    row = lambda cols: pl.BlockSpec((tm, cols), lambda i, pt: (i, 0))
    const = lambda shape: pl.BlockSpec(shape, lambda i, pt: (0,) * len(shape))
    single = lambda a: pl.BlockSpec(a.shape, lambda i, pt: (0,) * a.ndim,
                                    pipeline_mode=pl.Buffered(1))
    vec = lambda a: a.reshape(1, d)
    hbm = pl.BlockSpec(memory_space=pl.ANY)
    grid_spec = pltpu.PrefetchScalarGridSpec(
        num_scalar_prefetch=1,
        grid=(steps,),
        in_specs=[pl.BlockSpec(memory_space=pltpu.SMEM),
                  row(d), row(SB_WIDTH), row(GLA_V_WIDTH),
                  pl.BlockSpec((1, 6, 1, d), lambda i, pt: (i // tiles_per_seq, 0, 0, 0)),
                  single(wg), single(wua), single(wub), single(wo),
                  const((1, d)), const((1, d)), single(wf1), single(wf2),
                  const((1, d)), const((1, d)),
                  const(q_s.shape), const(lscan.shape), hbm, hbm],
        out_specs=(row(d), const((nseq, width))),
        scratch_shapes=[pltpu.VMEM((PAGE_SLOTS, p, width, LANES), F32),
                        pltpu.VMEM((PAGE_SLOTS, p, width, LANES), F32),
                        pltpu.SemaphoreType.DMA((PAGE_SLOTS,)),
                        pltpu.SemaphoreType.DMA((PAGE_SLOTS,)),
                        pltpu.VMEM((SB_HEADS, LANES), F32),
                        pltpu.VMEM((width, LANES), F32),
                        pltpu.VMEM((width, LANES), F32),
                        pltpu.VMEM((p * SB_HEADS, LANES), F32),
                        pltpu.VMEM((p * SB_HEADS, 2 * LANES), BF16)],
    )
    return pl.pallas_call(
        functools.partial(_post_paged_kernel, alpha, n_pages),
        grid_spec=grid_spec,
        out_shape=(jax.ShapeDtypeStruct((n, d), F32), jax.ShapeDtypeStruct((nseq, width), F32)),
        compiler_params=_cparams("arbitrary"),
        name="post_paged",
    )(page_table, bias, x2d, o_sb, o_g, mod, wg, wua, wub, wo, vec(ln1g), vec(ln1b), wf1, wf2,
      vec(ln2g), vec(ln2b), q_s, lscan, cache_kT, cache_vT)


def _gla_sample_kernel(gq_ref, gk_ref, la_ref, gv_ref, gr_ref, ng_ref, s0_ref, og_ref, st_ref):
    nh, dk, dv = GLA_HEADS, GLA_HEAD_K, GLA_HEAD_V
    for b in range(gq_ref.shape[0]):
        la = la_ref[b]
        q_dec = gq_ref[b] * (dk ** -0.5) * jnp.exp(la)
        k_inv = gk_ref[b] * jnp.exp(-la)
        k_end = gk_ref[b]
        s0 = s0_ref[b]
        v8 = gv_ref[b]
        qk = q_dec * k_inv
        qs = q_dec * s0
        outs = []
        for h in range(nh):
            rows = slice(h * dk, (h + 1) * dk)
            att = jnp.sum(qk[rows], axis=0, keepdims=True)
            outs.append(att * v8[h:h + 1] + jnp.sum(qs[rows], axis=0, keepdims=True))
        o = jnp.concatenate(outs + [jnp.zeros((8 - nh, dv), F32)], axis=0)
        o = o * lax.rsqrt(jnp.mean(o * o, axis=-1, keepdims=True) + GN_EPS) * ng_ref[...]
        r = gr_ref[b]
        og_ref[b] = o * (r * _sigmoid(r))
        v_rows = jnp.concatenate(
            [jnp.broadcast_to(v8[h:h + 1], (dk, dv)) for h in range(nh)], axis=0)
        st_ref[b] = jnp.exp(la) * s0 + k_end * v_rows


def _gla_sample(gq, gk, gv, la, gr, norm_g, s0):
    n = gq.shape[0]
    nh, dk, dv = GLA_HEADS, GLA_HEAD_K, GLA_HEAD_V
    kw = nh * dk
    pad_heads = lambda a: jnp.pad(a.reshape(n, nh, dv), ((0, 0), (0, 8 - nh), (0, 0)))
    per_step = GLA_SAMPLE_SEQS if n % GLA_SAMPLE_SEQS == 0 else 1
    colspec = pl.BlockSpec((per_step, kw, 1), lambda b: (b, 0, 0))
    headspec = pl.BlockSpec((per_step, 8, dv), lambda b: (b, 0, 0))
    statespec = pl.BlockSpec((per_step, kw, dv), lambda b: (b, 0, 0))
    og, st = pl.pallas_call(
        _gla_sample_kernel,
        grid=(n // per_step,),
        in_specs=[colspec, colspec, colspec, headspec, headspec, _const_spec((1, dv)), statespec],
        out_specs=(headspec, statespec),
        out_shape=(jax.ShapeDtypeStruct((n, 8, dv), F32),
                   jax.ShapeDtypeStruct((n, kw, dv), F32)),
        compiler_params=_cparams("parallel"),
        name="gla_sample",
    )(gq.reshape(n, kw, 1), gk.reshape(n, kw, 1), la.reshape(n, kw, 1),
      pad_heads(gv), pad_heads(gr), norm_g.reshape(1, dv), s0)
    return og[:, :nh].reshape(n, nh * dv), st


def kernel(x_prompt, x_sample, cache_k, cache_v, state_gla, page_table, c_prompt, c_sample,
           w_ada, b_ada, w_in, sb_bias, w_gla_g2, b_gla_g, gla_norm_g, w_up_a, w_up_b, w_o,
           ln1_g, ln1_b, w_ff1, w_ff2, ln2_g, ln2_b):
    depth = w_ada.shape[0]
    pb, seq_len, d = x_prompt.shape
    sb_n = x_sample.shape[0]
    assert x_sample.shape[1] == 1
    alpha = (2.0 * depth) ** 0.25
    n_phys, page_size = cache_k.shape[1], cache_k.shape[2]

    yp = x_prompt.reshape(pb * seq_len, d)
    ys = x_sample.reshape(sb_n, d)
    c_all = jnp.concatenate([c_prompt, c_sample], axis=0)
    main_w = 3 * SB_WIDTH + 2 * GLA_K_WIDTH + 2 * GLA_V_WIDTH
    gate_off = main_w + GLA_GATE_RANK

    outs = [[] for _ in range(6)]
    for l in range(depth):
        ada = _ada(c_all, w_ada[l].astype(BF16), b_ada[l])
        mod_p = ada[:pb].reshape(pb, 6, 1, d)
        mod_s = ada[pb:].reshape(sb_n, 6, d).transpose(1, 0, 2).reshape(1, 6, sb_n, d)

        w_in_l = w_in[l]
        wm = w_in_l[:, :main_w].astype(BF16)
        wl = jnp.pad(w_in_l[:, main_w:gate_off], ((0, 0), (0, LANES - GLA_GATE_RANK))).astype(BF16)
        wg = w_in_l[:, gate_off:].astype(BF16)
        wg2 = jnp.pad(w_gla_g2[l], ((0, LANES - GLA_GATE_RANK), (0, 0))).astype(BF16)
        bg = b_gla_g[l].reshape(1, GLA_K_WIDTH)
        post_w = (wg, w_up_a[l].astype(BF16), w_up_b[l].astype(BF16), w_o[l].astype(BF16),
                  ln1_g[l], ln1_b[l], w_ff1[l].astype(BF16), w_ff2[l].astype(BF16),
                  ln2_g[l], ln2_b[l])

        qT3, k_p, v_p, k_bf, vT3, gq, gk, gv, gr, la = _proj_prompt(yp, mod_p, seq_len, wm, wl, wg2, bg)
        q_s, k_s, v_s, gq_s, gk_s, gv_s, gr_s, la_s = _proj_sample(ys, mod_s, wm, wl, wg2, bg)
        o_sb = _sb_prompt(sb_bias[l], qT3, k_bf, vT3, pb, seq_len)
        o_g, st_p = _gla_prompt(gq, gk, gv, la, gr, gla_norm_g[l], pb, seq_len)

        page_major = lambda c: jnp.transpose(c, (0, 2, 3, 1)).reshape(n_phys, SB_WIDTH, page_size)
        yp, o_sb_s = _post_with_paged_attention(
            yp, o_sb, o_g, mod_p, seq_len, alpha, post_w, page_table, sb_bias[l], q_s,
            page_major(cache_k[l]), page_major(cache_v[l]))

        s0 = state_gla[l].reshape(sb_n, GLA_HEADS * GLA_HEAD_K, GLA_HEAD_V)
        o_g_s, st_s = _gla_sample(gq_s, gk_s, gv_s, la_s, gr_s, gla_norm_g[l], s0)
        ys = _post(ys, o_sb_s, o_g_s, mod_s, 1, alpha, *post_w)

        rows_major = lambda t: jnp.transpose(
            t.reshape(pb, SB_HEADS, SB_HEAD_DIM, seq_len), (0, 3, 1, 2))
        outs[0].append(rows_major(k_p))
        outs[1].append(rows_major(v_p))
        outs[2].append(st_p.reshape(pb, GLA_HEADS, GLA_HEAD_K, GLA_HEAD_V))
        outs[3].append(k_s.reshape(sb_n, 1, SB_HEADS, SB_HEAD_DIM))
        outs[4].append(v_s.reshape(sb_n, 1, SB_HEADS, SB_HEAD_DIM))
        outs[5].append(st_s.reshape(sb_n, GLA_HEADS, GLA_HEAD_K, GLA_HEAD_V))

    stacked = [jnp.stack(o) for o in outs]
    return (yp.reshape(pb, seq_len, d), ys.reshape(sb_n, 1, d),
            stacked[0], stacked[1], stacked[2].astype(state_gla.dtype),
            stacked[3], stacked[4], stacked[5].astype(state_gla.dtype))
```
